```python
import jax, jax.numpy as jnp
from jax import lax
import numpy as np

D_MODEL = 1024
BATCH = 2
SEQ = 8192
DEPTH = 2

A_HEAD_DIM = 64
A_HEADS = D_MODEL // A_HEAD_DIM
A_GROUP_SIZE = 4
A_KV_HEADS = A_HEADS // A_GROUP_SIZE
A_WIDTH = A_HEADS * A_HEAD_DIM
A_KV_WIDTH = A_KV_HEADS * A_HEAD_DIM
CMP_BLOCK = 32
CMP_STRIDE = 16
SEL_BLOCK = 64
N_SELECT = 16
WINDOW = 512
Q_BLOCK = 128
FORCE_SCORE = 1.0e4
POOL_WINDOWS = (2, 4, 8, 16)
B_WIDTH = D_MODEL
POOL_GROUP = B_WIDTH // len(POOL_WINDOWS)
HG_EXPAND = 128
C_WIDTH = D_MODEL
HG_HEADS = C_WIDTH // HG_EXPAND
HG_VDIM = C_WIDTH // HG_HEADS
HG_CHUNK = 64
N_BRANCH = 3
BRANCH_WIDTH = D_MODEL
NORM_EPS = 1e-6

SPLIT_SIZES = (
    A_WIDTH,
    A_KV_WIDTH, A_KV_WIDTH,
    A_KV_WIDTH, A_KV_WIDTH,
    A_KV_WIDTH, A_KV_WIDTH,
    3 * A_HEADS,
    A_WIDTH,
    B_WIDTH, B_WIDTH,
    C_WIDTH, C_WIDTH, C_WIDTH, C_WIDTH,
    N_BRANCH * D_MODEL,
)
SPLIT_POINTS = tuple(int(v) for v in np.cumsum(SPLIT_SIZES)[:-1])
N_IN = int(sum(SPLIT_SIZES))

kernel_name = 'hybrid_nsa_pool_hgrn2_block'


def rms_norm(x, gain):
    xf = x.astype(jnp.float32)
    y = xf * lax.rsqrt(jnp.mean(xf * xf, axis=-1, keepdims=True) + NORM_EPS)
    return (y * gain.astype(jnp.float32)).astype(x.dtype)


def masked_softmax(s, mask):
    s = jnp.where(mask, s.astype(jnp.float32), -jnp.inf)
    m = jnp.max(s, axis=-1, keepdims=True)
    m = jnp.where(jnp.isfinite(m), m, 0.0)
    p = jnp.exp(s - m)
    return p / jnp.maximum(jnp.sum(p, axis=-1, keepdims=True), 1e-30)


def nsa_mixer(q, kc, vc, ks, vs, kw, vw, gate_logits, pe_k, pe_v, w_ck, w_cv):
    B, T, G, HPG, DH = q.shape
    scale = DH ** -0.5
    n_c = (T - CMP_BLOCK) // CMP_STRIDE + 1
    c_start = jnp.arange(n_c) * CMP_STRIDE
    cidx = c_start[:, None] + jnp.arange(CMP_BLOCK)[None, :]
    ck = jnp.einsum('bcpgd,pde->bcge', kc[:, cidx] + pe_k[:, None, :], w_ck)
    cv = jnp.einsum('bcpgd,pde->bcge', vc[:, cidx] + pe_v[:, None, :], w_cv)
    c_end = c_start + CMP_BLOCK - 1
    n_s = T // SEL_BLOCK
    n_sel = min(N_SELECT, n_s)
    s_start = jnp.arange(n_s) * SEL_BLOCK
    overlap = jnp.clip(jnp.minimum(c_start[:, None] + CMP_BLOCK, s_start[None, :] + SEL_BLOCK)
                       - jnp.maximum(c_start[:, None], s_start[None, :]), 0).astype(jnp.float32) / CMP_BLOCK
    ksb = ks.reshape(B, n_s, SEL_BLOCK, G, DH).transpose(0, 3, 1, 2, 4)
    vsb = vs.reshape(B, n_s, SEL_BLOCK, G, DH).transpose(0, 3, 1, 2, 4)
    b_idx = jnp.arange(B)[:, None, None, None]
    g_idx = jnp.arange(G)[None, :, None, None]
    blk = jnp.arange(n_s)
    kw_pad = jnp.pad(kw, ((0, 0), (WINDOW, 0), (0, 0), (0, 0)))
    vw_pad = jnp.pad(vw, ((0, 0), (WINDOW, 0), (0, 0), (0, 0)))

    def block(i):
        s0 = i * Q_BLOCK
        t = s0 + jnp.arange(Q_BLOCK)
        qb = lax.dynamic_slice_in_dim(q, s0, Q_BLOCK, axis=1)
        gb = jax.nn.sigmoid(lax.dynamic_slice_in_dim(gate_logits, s0, Q_BLOCK, axis=1).astype(jnp.float32))
        sc = jnp.einsum('bqghd,bcgd->bghqc', qb, ck) * scale
        pc = masked_softmax(sc, c_end[None, :] <= t[:, None])
        oc = jnp.einsum('bghqc,bcgd->bqghd', pc, cv)
        imp = jnp.einsum('bghqc,cj->bgqj', pc, overlap)
        cur = t // SEL_BLOCK
        valid = blk[None, :] <= cur[:, None]
        force = (blk[None, :] == 0) | (blk[None, :] == cur[:, None]) | (blk[None, :] == cur[:, None] - 1)
        score = jnp.where(valid, jnp.where(force, FORCE_SCORE, imp), -1.0)
        _, idx = lax.top_k(score, n_sel)
        kg = ksb[b_idx, g_idx, idx]
        vg = vsb[b_idx, g_idx, idx]
        pos = idx[..., None] * SEL_BLOCK + jnp.arange(SEL_BLOCK)
        ss = jnp.einsum('bqghd,bgqnkd->bghqnk', qb, kg) * scale
        ms = (pos <= t[None, None, :, None, None]).reshape(B, G, 1, Q_BLOCK, n_sel * SEL_BLOCK)
        ps = masked_softmax(ss.reshape(B, G, HPG, Q_BLOCK, n_sel * SEL_BLOCK), ms)
        o_sel = jnp.einsum('bghqm,bgqmd->bqghd', ps, vg.reshape(B, G, Q_BLOCK, n_sel * SEL_BLOCK, DH))
        kwb = lax.dynamic_slice_in_dim(kw_pad, s0, Q_BLOCK + WINDOW, axis=1)
        vwb = lax.dynamic_slice_in_dim(vw_pad, s0, Q_BLOCK + WINDOW, axis=1)
        p = s0 - WINDOW + jnp.arange(Q_BLOCK + WINDOW)
        diff = t[:, None] - p[None, :]
        mw = (diff >= 0) & (diff < WINDOW) & (p[None, :] >= 0)
        sw = jnp.einsum('bqghd,bkgd->bghqk', qb, kwb) * scale
        pw = masked_softmax(sw, mw)
        ow = jnp.einsum('bghqk,bkgd->bqghd', pw, vwb)
        return gb[..., 0:1] * oc + gb[..., 1:2] * o_sel + gb[..., 2:3] * ow

    out = lax.map(block, jnp.arange(T // Q_BLOCK))
    return out.transpose(1, 0, 2, 3, 4, 5).reshape(B, T, G * HPG * DH).astype(q.dtype)


def pool_mixer(v, w_pool, scale):
    B, T, _ = v.shape
    vf = v.astype(jnp.float32)
    c = jnp.cumsum(vf, axis=1)
    count = jnp.arange(1, T + 1).astype(jnp.float32)[None, :, None]
    outs = []
    for g, w in enumerate(POOL_WINDOWS):
        sl = slice(g * POOL_GROUP, (g + 1) * POOL_GROUP)
        cg = c[..., sl]
        lo = jnp.pad(cg, ((0, 0), (w, 0), (0, 0)))[:, :T]
        outs.append((cg - lo) / jnp.minimum(count, float(w)) - vf[..., sl])
    pooled = jnp.stack(outs, axis=2)
    mixed = jnp.einsum('btgc,gcd->btgd', pooled, w_pool.astype(jnp.float32)).reshape(B, T, B_WIDTH)
    return (mixed * scale.astype(jnp.float32)).astype(v.dtype)


def hgrn2_mixer(q, f_logits, inp, lb):
    B, T, _ = q.shape
    H, DK, DV, C = HG_HEADS, HG_EXPAND, HG_VDIM, HG_CHUNK
    NC = T // C
    lb = lb.reshape(H, DK)
    f = lb + (1.0 - lb) * jax.nn.sigmoid(f_logits.astype(jnp.float32).reshape(B, T, H, DK))
    logf = jnp.log(f)
    k = 1.0 - f

    def chunks(a, d):
        return a.astype(jnp.float32).reshape(B, NC, C, H, d).transpose(1, 0, 3, 2, 4)

    qc, kc, lfc = chunks(q, DK), chunks(k, DK), chunks(logf, DK)
    vc = chunks(inp, DV)
    tri = jnp.tril(jnp.ones((C, C), dtype=bool))

    def step(S, xs):
        qq, kk, lf, vv = xs
        A = jnp.cumsum(lf, axis=2)
        o_inter = jnp.einsum('bhtd,bhde->bhte', qq * jnp.exp(A), S)
        rel = jnp.where(tri[:, :, None], A[:, :, :, None, :] - A[:, :, None, :, :], -jnp.inf)
        att = jnp.einsum('bhtd,bhsd,bhtsd->bhts', qq, kk, jnp.exp(rel))
        o_intra = jnp.einsum('bhts,bhse->bhte', att, vv)
        A_last = A[:, :, -1:, :]
        S = jnp.exp(A_last[:, :, 0, :])[..., None] * S + jnp.einsum('bhsd,bhse->bhde', kk * jnp.exp(A_last - A), vv)
        return S, o_inter + o_intra

    S0 = jnp.zeros((B, H, DK, DV), jnp.float32)
    _, o = lax.scan(step, S0, (qc, kc, lfc, vc))
    return o.transpose(1, 0, 3, 2, 4).reshape(B, T, H, DV).astype(q.dtype)


def setup_inputs(seed: int = 0) -> dict:
    key = jax.random.key(seed)
    ks = jax.random.split(key, 14)

    def nrm(k, shape, s):
        return jax.random.normal(k, shape, jnp.float32) * s

    return {
        'x': nrm(ks[0], (BATCH, SEQ, D_MODEL), 1.0),
        'pre_norm': 1.0 + nrm(ks[1], (DEPTH, D_MODEL), 0.05),
        'w_in': nrm(ks[2], (DEPTH, D_MODEL, N_IN), D_MODEL ** -0.5),
        'cmp_pe_k': nrm(ks[3], (DEPTH, CMP_BLOCK, A_HEAD_DIM), 0.1),
        'cmp_pe_v': nrm(ks[4], (DEPTH, CMP_BLOCK, A_HEAD_DIM), 0.1),
        'cmp_w_k': nrm(ks[5], (DEPTH, CMP_BLOCK, A_HEAD_DIM, A_HEAD_DIM), (CMP_BLOCK * A_HEAD_DIM) ** -0.5),
        'cmp_w_v': nrm(ks[6], (DEPTH, CMP_BLOCK, A_HEAD_DIM, A_HEAD_DIM), (CMP_BLOCK * A_HEAD_DIM) ** -0.5),
        'pool_w': nrm(ks[7], (DEPTH, len(POOL_WINDOWS), POOL_GROUP, POOL_GROUP), POOL_GROUP ** -0.5),
        'pool_scale': 1.0 + nrm(ks[8], (DEPTH, B_WIDTH), 0.1),
        'hgrn_lb_logits': nrm(ks[9], (DEPTH, C_WIDTH), 1.0),
        'hgrn_norm': 1.0 + nrm(ks[10], (DEPTH, C_WIDTH), 0.05),
        'w_branch': nrm(ks[11], (DEPTH, N_BRANCH, BRANCH_WIDTH, D_MODEL), BRANCH_WIDTH ** -0.5),
        'w_out': nrm(ks[12], (DEPTH, D_MODEL, D_MODEL), D_MODEL ** -0.5),
        'post_norm': 1.0 + nrm(ks[13], (DEPTH, D_MODEL), 0.05),
    }


def reference(x, pre_norm, w_in, cmp_pe_k, cmp_pe_v, cmp_w_k, cmp_w_v, pool_w, pool_scale,
              hgrn_lb_logits, hgrn_norm, w_branch, w_out, post_norm):
    B, T, _ = x.shape
    probs = jax.nn.softmax(hgrn_lb_logits.astype(jnp.float32), axis=0)
    lbs = jnp.cumsum(probs, axis=0) - probs[0]

    def kv(a):
        return a.reshape(B, T, A_KV_HEADS, A_HEAD_DIM)

    for l in range(DEPTH):
        h = rms_norm(x, pre_norm[l])
        z = jnp.einsum('btd,dn->btn', h, w_in[l])
        (a_q, a_kc, a_vc, a_ks, a_vs, a_kw, a_vw, a_gate, a_silu,
         b_v, b_silu, c_q, c_f, c_i, c_silu, merge_logits) = jnp.split(z, SPLIT_POINTS, axis=-1)
        o_a = nsa_mixer(a_q.reshape(B, T, A_KV_HEADS, A_GROUP_SIZE, A_HEAD_DIM),
                        kv(a_kc), kv(a_vc), kv(a_ks), kv(a_vs), kv(a_kw), kv(a_vw),
                        a_gate.reshape(B, T, A_KV_HEADS, A_GROUP_SIZE, 3),
                        cmp_pe_k[l], cmp_pe_v[l], cmp_w_k[l], cmp_w_v[l])
        o_a = o_a * jax.nn.silu(a_silu)
        o_b = pool_mixer(b_v, pool_w[l], pool_scale[l]) * jax.nn.silu(b_silu)
        o_c = hgrn2_mixer(c_q, c_f, c_i, lbs[l])
        o_c = rms_norm(o_c, hgrn_norm[l].reshape(HG_HEADS, HG_VDIM)).reshape(B, T, C_WIDTH) * jax.nn.silu(c_silu)
        branches = jnp.stack([o_a, o_b, o_c], axis=2)
        y = jnp.einsum('btnc,ncd->btnd', branches, w_branch[l])
        gates = jax.nn.sigmoid(merge_logits.reshape(B, T, N_BRANCH, D_MODEL))
        merged = jnp.sum(gates * y, axis=2)
        out = jnp.einsum('btd,de->bte', merged, w_out[l])
        x = x + rms_norm(out, post_norm[l])
    return x
```

```python
import functools

import jax
import jax.numpy as jnp
from jax import lax
from jax.experimental import pallas as pl
from jax.experimental.pallas import tpu as pltpu

F32 = jnp.float32
BF16 = jnp.bfloat16

D_MODEL = 1024
NORM_EPS = 1e-6
DH = 64
N_G = 4
HPG = 4
GW = HPG * DH
KVW = N_G * DH
CMP_BLOCK = 32
CMP_STRIDE = 16
CMP_LANES = 128
SEL_BLOCK = 64
N_SELECT = 16
WINDOW = 512
QT = 128
FORCE_SCORE = 1.0e4
ATT_SCALE = DH ** -0.5
MASK_BIAS = -(2.0 ** 30)
POOL_WINDOWS = (2, 4, 8, 16)
POOL_GROUP = D_MODEL // len(POOL_WINDOWS)
POOL_HALO = 16
HG_HEADS = 8
HG_DK = 128
HG_CHUNK = 64
HG_SUB = 16

Z16_Q, Z16_KS, Z16_VS, Z16_KW, Z16_VW = 0, 1024, 1280, 1536, 1792
Z16_ASILU, Z16_BV, Z16_BSILU = 2048, 3072, 4096
Z16_CQ, Z16_CI, Z16_CSILU, Z16_MERGE = 5120, 6144, 7168, 8192
Z16_W = 11264
Z32_CF, Z32_KC, Z32_VC, Z32_GATE = 0, 1024, 1280, 1536
Z32_W = 2048
GATE_PAD = 128

VMEM_LIMIT = 56 * 1024 * 1024


def _cparams(sem):
    return pltpu.CompilerParams(dimension_semantics=sem, vmem_limit_bytes=VMEM_LIMIT)


def _sigmoid(x):
    return 1.0 / (1.0 + jnp.exp(-x))


def _silu(x):
    return x * _sigmoid(x)


def _inproj_kernel(x_ref, g_ref, w_ref, o_ref, h_ref):
    @pl.when(pl.program_id(1) == 0)
    def _():
        x = x_ref[...]
        ms = jnp.mean(x * x, axis=-1, keepdims=True)
        h_ref[...] = (x * lax.rsqrt(ms + NORM_EPS) * g_ref[...]).astype(BF16)

    o_ref[...] = jnp.dot(h_ref[...], w_ref[...], preferred_element_type=F32).astype(o_ref.dtype)


def _inproj(x2, gain, w, out_dtype, tm, tn):
    m, d = x2.shape
    n = w.shape[1]
    return pl.pallas_call(
        _inproj_kernel,
        grid=(m // tm, n // tn),
        in_specs=[pl.BlockSpec((tm, d), lambda i, j: (i, 0)),
                  pl.BlockSpec((1, d), lambda i, j: (0, 0)),
                  pl.BlockSpec((d, tn), lambda i, j: (0, j))],
        out_specs=pl.BlockSpec((tm, tn), lambda i, j: (i, j)),
        out_shape=jax.ShapeDtypeStruct((m, n), out_dtype),
        scratch_shapes=[pltpu.VMEM((tm, d), BF16)],
        compiler_params=_cparams(("parallel", "arbitrary")),
        name="inproj",
    )(x2, gain, w)


def _compress_kernel(x_ref, pe_ref, w_ref, o_ref, ot_ref, *, nch):
    acc_a = jnp.zeros((nch, CMP_LANES), F32)
    acc_b = jnp.zeros((nch, CMP_LANES), F32)
    for p in range(CMP_STRIDE):
        xp = x_ref[0, pl.ds(p, nch, stride=CMP_STRIDE), :]
        a = (xp + pe_ref[0, p:p + 1, :]).astype(BF16)
        b = (xp + pe_ref[0, CMP_STRIDE + p:CMP_STRIDE + p + 1, :]).astype(BF16)
        acc_a += jnp.dot(a, w_ref[0, p], preferred_element_type=F32)
        acc_b += jnp.dot(b, w_ref[0, CMP_STRIDE + p], preferred_element_type=F32)
    c = acc_a + pltpu.roll(acc_b, nch - 1, 0)
    ct = c.T
    for g in range(CMP_LANES // DH):
        o_ref[0, 0, g] = c[:, g * DH:(g + 1) * DH].astype(BF16)
        ot_ref[0, 0, g] = ct[g * DH:(g + 1) * DH, :].astype(BF16)


def _compress(z32, pe, wbd, batch, seq):
    nch = seq // CMP_STRIDE
    z3 = z32.reshape(batch, seq, Z32_W)
    gpb = CMP_LANES // DH
    return pl.pallas_call(
        functools.partial(_compress_kernel, nch=nch),
        grid=(batch, 2, N_G // gpb),
        in_specs=[pl.BlockSpec((1, seq, CMP_LANES),
                               lambda b, s, c: (b, 0, (Z32_KC + s * KVW) // CMP_LANES + c)),
                  pl.BlockSpec((1, CMP_BLOCK, CMP_LANES), lambda b, s, c: (s, 0, 0)),
                  pl.BlockSpec((1, CMP_BLOCK, CMP_LANES, CMP_LANES), lambda b, s, c: (s, 0, 0, 0))],
        out_specs=[pl.BlockSpec((1, 1, gpb, nch, DH), lambda b, s, c: (b, s, c, 0, 0)),
                   pl.BlockSpec((1, 1, gpb, DH, nch), lambda b, s, c: (b, s, c, 0, 0))],
        out_shape=[jax.ShapeDtypeStruct((batch, 2, N_G, nch, DH), BF16),
                   jax.ShapeDtypeStruct((batch, 2, N_G, DH, nch), BF16)],
        compiler_params=_cparams(("parallel", "parallel", "parallel")),
        name="compress",
    )(z3, pe, wbd)


def _cmp_select_kernel(q_ref, ck_ref, cvt_ref, ov_ref, gate_ref, oc_ref, sb_ref, *, nch, n_s, n_sel):
    s0 = pl.program_id(2) * QT
    qt = q_ref[0].astype(F32).T
    ck = ck_ref[0, 0, 0]
    cvt = cvt_ref[0, 0, 0]
    t_lane = s0 + lax.broadcasted_iota(jnp.int32, (1, QT), 1)
    c_end = lax.broadcasted_iota(jnp.int32, (nch, 1), 0) * CMP_STRIDE + (CMP_BLOCK - 1)
    visible = c_end <= t_lane
    gates = _sigmoid(gate_ref[0])

    p_sum = jnp.zeros((nch, QT), F32)
    outs = []
    for h in range(HPG):
        qh = qt[h * DH:(h + 1) * DH, :].astype(BF16)
        s = jnp.dot(ck, qh, preferred_element_type=F32) * ATT_SCALE
        s = jnp.where(visible, s, -jnp.inf)
        m = jnp.max(s, axis=0, keepdims=True)
        m = jnp.where(m == -jnp.inf, 0.0, m)
        e = jnp.exp(s - m)
        p = e / jnp.maximum(jnp.sum(e, axis=0, keepdims=True), 1e-30)
        p_sum += p
        oct_h = jnp.dot(cvt, p.astype(BF16), preferred_element_type=F32)
        outs.append(oct_h.T * gates[:, 3 * h:3 * h + 1])
    oc_ref[0] = jnp.concatenate(outs, axis=1).astype(BF16)

    p_hi = p_sum.astype(BF16)
    p_lo = (p_sum - p_hi.astype(F32)).astype(BF16)
    ov = ov_ref[...]
    imp = (jnp.dot(ov, p_hi, preferred_element_type=F32)
           + jnp.dot(ov, p_lo, preferred_element_type=F32))

    blk = lax.broadcasted_iota(jnp.int32, (n_s, 1), 0)
    cur = t_lane // SEL_BLOCK
    valid = blk <= cur
    force = (blk == 0) | (blk == cur) | (blk == cur - 1)
    score = jnp.where(valid, jnp.where(force, FORCE_SCORE, imp), -1.0)

    def pick(_, carry):
        sc, sel = carry
        m = jnp.max(sc, axis=0, keepdims=True)
        first = jnp.min(jnp.where(sc == m, blk, n_s), axis=0, keepdims=True)
        hit = blk == first
        return jnp.where(hit, -jnp.inf, sc), jnp.where(hit, 1.0, sel)

    _, sel = lax.fori_loop(0, n_sel, pick, (score, jnp.zeros((n_s, QT), F32)), unroll=True)
    bias_t = jnp.where((sel > 0.5) & valid, 0.0, MASK_BIAS)
    sb_ref[0, 0] = bias_t.T.astype(BF16)


def _cmp_select(z16, z32, ck, cvt, ov, batch, seq):
    nch = seq // CMP_STRIDE
    n_s = seq // SEL_BLOCK
    n_sel = min(N_SELECT, n_s)
    z3 = z16.reshape(batch, seq, Z16_W)
    g3 = z32.reshape(batch, seq, Z32_W)
    return pl.pallas_call(
        functools.partial(_cmp_select_kernel, nch=nch, n_s=n_s, n_sel=n_sel),
        grid=(batch, N_G, seq // QT),
        in_specs=[pl.BlockSpec((1, QT, GW), lambda b, g, i: (b, i, Z16_Q // GW + g)),
                  pl.BlockSpec((1, 1, 1, nch, DH), lambda b, g, i: (b, 0, g, 0, 0)),
                  pl.BlockSpec((1, 1, 1, DH, nch), lambda b, g, i: (b, 1, g, 0, 0)),
                  pl.BlockSpec((n_s, nch), lambda b, g, i: (0, 0)),
                  pl.BlockSpec((1, QT, GATE_PAD), lambda b, g, i: (b, i, Z32_GATE // GATE_PAD + g))],
        out_specs=[pl.BlockSpec((1, QT, GW), lambda b, g, i: (b, i, g)),
                   pl.BlockSpec((1, 1, QT, n_s), lambda b, g, i: (b, g, i, 0))],
        out_shape=[jax.ShapeDtypeStruct((batch, seq, D_MODEL), BF16),
                   jax.ShapeDtypeStruct((batch, N_G, seq, n_s), BF16)],
        compiler_params=_cparams(("parallel", "parallel", "parallel")),
        name="cmp_select",
    )(z3, ck, cvt, ov, g3)


def _kv_prep_kernel(ks_ref, vs_ref, kw_ref, vw_ref, ka_ref, va_ref, kwt_ref, vwa_ref, *, n_s, tt):
    t0 = pl.program_id(1) * tt
    kst = ks_ref[0].astype(F32).T
    kwt = kw_ref[0].astype(F32).T
    vs = vs_ref[0]
    vw = vw_ref[0]
    pos = t0 + lax.broadcasted_iota(jnp.int32, (1, tt), 1)
    blk = lax.broadcasted_iota(jnp.int32, (n_s, 1), 0)
    onehot = jnp.where(pos // SEL_BLOCK == blk, 1.0, 0.0).astype(BF16)
    ones_col = jnp.where(lax.broadcasted_iota(jnp.int32, (tt, DH), 1) == 0, 1.0, 0.0).astype(BF16)
    for g in range(N_G):
        ka_ref[0, g, 0:n_s, :] = onehot
        ka_ref[0, g, n_s:n_s + DH, :] = kst[g * DH:(g + 1) * DH, :].astype(BF16)
        kwt_ref[0, g] = kwt[g * DH:(g + 1) * DH, :].astype(BF16)
        va_ref[0, g] = jnp.concatenate([vs[:, g * DH:(g + 1) * DH], ones_col], axis=1)
        vwa_ref[0, g] = jnp.concatenate([vw[:, g * DH:(g + 1) * DH], ones_col], axis=1)


def _kv_prep(z16, batch, seq, tt):
    n_s = seq // SEL_BLOCK
    z3 = z16.reshape(batch, seq, Z16_W)

    def col(off):
        return pl.BlockSpec((1, tt, KVW), lambda b, i: (b, i, off // KVW))

    return pl.pallas_call(
        functools.partial(_kv_prep_kernel, n_s=n_s, tt=tt),
        grid=(batch, seq // tt),
        in_specs=[col(Z16_KS), col(Z16_VS), col(Z16_KW), col(Z16_VW)],
        out_specs=[pl.BlockSpec((1, N_G, n_s + DH, tt), lambda b, i: (b, 0, 0, i)),
                   pl.BlockSpec((1, N_G, tt, 2 * DH), lambda b, i: (b, 0, i, 0)),
                   pl.BlockSpec((1, N_G, DH, tt), lambda b, i: (b, 0, 0, i)),
                   pl.BlockSpec((1, N_G, tt, 2 * DH), lambda b, i: (b, 0, i, 0))],
        out_shape=[jax.ShapeDtypeStruct((batch, N_G, n_s + DH, seq), BF16),
                   jax.ShapeDtypeStruct((batch, N_G, seq, 2 * DH), BF16),
                   jax.ShapeDtypeStruct((batch, N_G, DH, seq), BF16),
                   jax.ShapeDtypeStruct((batch, N_G, seq, 2 * DH), BF16)],
        compiler_params=_cparams(("parallel", "parallel")),
        name="kv_prep",
    )(z3, z3, z3, z3)


def _sel_attn_kernel(q_ref, sb_ref, gate_ref, k_ref, v_ref, o_ref, qa_ref, m_ref, acc_ref, *, n_s, tk):
    s0 = pl.program_id(2) * QT
    q = q_ref[0]
    sb = sb_ref[0, 0]
    for h in range(HPG):
        qa_ref[h * QT:(h + 1) * QT, 0:n_s] = sb
        qa_ref[h * QT:(h + 1) * QT, n_s:n_s + DH] = (
            q[:, h * DH:(h + 1) * DH].astype(F32) * ATT_SCALE).astype(BF16)
    m_ref[...] = jnp.full(m_ref.shape, -jnp.inf, F32)
    acc_ref[...] = jnp.zeros(acc_ref.shape, F32)
    n_full = s0 // tk

    def step(kt, causal):
        off = pl.multiple_of(kt * tk, tk)
        s = jnp.dot(qa_ref[...], k_ref[0, 0, :, pl.ds(off, tk)], preferred_element_type=F32)
        if causal:
            kpos = off + lax.broadcasted_iota(jnp.int32, (1, tk), 1)
            t = s0 + lax.broadcasted_iota(jnp.int32, (HPG * QT, 1), 0) % QT
            s = jnp.where(kpos <= t, s, MASK_BIAS)
        m_prev = m_ref[...]
        m_new = jnp.maximum(m_prev, jnp.max(s, axis=1, keepdims=True))
        p = jnp.exp(s - m_new)
        acc_ref[...] = jnp.exp(m_prev - m_new) * acc_ref[...] + jnp.dot(
            p.astype(BF16), v_ref[0, 0, pl.ds(off, tk), :], preferred_element_type=F32)
        m_ref[...] = m_new

    step(n_full, True)

    def body(kt, carry):
        step(kt, False)
        return carry

    lax.fori_loop(0, n_full, body, 0)

    gates = _sigmoid(gate_ref[0])
    outs = []
    for h in range(HPG):
        a = acc_ref[h * QT:(h + 1) * QT, :]
        outs.append(a[:, 0:DH] / a[:, DH:DH + 1] * gates[:, 3 * h + 1:3 * h + 2])
    o_ref[0] = jnp.concatenate(outs, axis=1).astype(BF16)


def _sel_attn(z16, z32, sbias, kaug, vaug, batch, seq, tk):
    n_s = seq // SEL_BLOCK
    z3 = z16.reshape(batch, seq, Z16_W)
    g3 = z32.reshape(batch, seq, Z32_W)
    return pl.pallas_call(
        functools.partial(_sel_attn_kernel, n_s=n_s, tk=tk),
        grid=(batch, N_G, seq // QT),
        in_specs=[pl.BlockSpec((1, QT, GW), lambda b, g, i: (b, i, Z16_Q // GW + g)),
                  pl.BlockSpec((1, 1, QT, n_s), lambda b, g, i: (b, g, i, 0)),
                  pl.BlockSpec((1, QT, GATE_PAD), lambda b, g, i: (b, i, Z32_GATE // GATE_PAD + g)),
                  pl.BlockSpec((1, 1, n_s + DH, seq), lambda b, g, i: (b, g, 0, 0)),
                  pl.BlockSpec((1, 1, seq, 2 * DH), lambda b, g, i: (b, g, 0, 0))],
        out_specs=pl.BlockSpec((1, QT, GW), lambda b, g, i: (b, i, g)),
        out_shape=jax.ShapeDtypeStruct((batch, seq, D_MODEL), BF16),
        scratch_shapes=[pltpu.VMEM((HPG * QT, n_s + DH), BF16),
                        pltpu.VMEM((HPG * QT, 1), F32),
                        pltpu.VMEM((HPG * QT, 2 * DH), F32)],
        compiler_params=_cparams(("parallel", "parallel", "arbitrary")),
        name="sel_attn",
    )(z3, sbias, g3, kaug, vaug)


def _win_attn_kernel(q_ref, gate_ref, k_ref, v_ref, o_ref, *, span):
    s0 = pl.program_id(2) * QT
    start = pl.multiple_of(jnp.maximum(s0 - WINDOW, 0), QT)
    q = q_ref[0]
    qs = jnp.concatenate(
        [(q[:, h * DH:(h + 1) * DH].astype(F32) * ATT_SCALE).astype(BF16) for h in range(HPG)], axis=0)
    s = jnp.dot(qs, k_ref[0, 0, :, pl.ds(start, span)], preferred_element_type=F32)
    kpos = start + lax.broadcasted_iota(jnp.int32, (1, span), 1)
    t = s0 + lax.broadcasted_iota(jnp.int32, (HPG * QT, 1), 0) % QT
    s = jnp.where((kpos <= t) & (t - kpos < WINDOW), s, -jnp.inf)
    e = jnp.exp(s - jnp.max(s, axis=1, keepdims=True))
    a = jnp.dot(e.astype(BF16), v_ref[0, 0, pl.ds(start, span), :], preferred_element_type=F32)
    gates = _sigmoid(gate_ref[0])
    outs = []
    for h in range(HPG):
        ah = a[h * QT:(h + 1) * QT, :]
        outs.append(ah[:, 0:DH] / ah[:, DH:DH + 1] * gates[:, 3 * h + 2:3 * h + 3])
    o_ref[0] = jnp.concatenate(outs, axis=1).astype(BF16)


def _win_attn(z16, z32, kwt, vwaug, batch, seq):
    span = min(WINDOW + QT, seq)
    z3 = z16.reshape(batch, seq, Z16_W)
    g3 = z32.reshape(batch, seq, Z32_W)
    return pl.pallas_call(
        functools.partial(_win_attn_kernel, span=span),
        grid=(batch, N_G, seq // QT),
        in_specs=[pl.BlockSpec((1, QT, GW), lambda b, g, i: (b, i, Z16_Q // GW + g)),
                  pl.BlockSpec((1, QT, GATE_PAD), lambda b, g, i: (b, i, Z32_GATE // GATE_PAD + g)),
                  pl.BlockSpec((1, 1, DH, seq), lambda b, g, i: (b, g, 0, 0)),
                  pl.BlockSpec((1, 1, seq, 2 * DH), lambda b, g, i: (b, g, 0, 0))],
        out_specs=pl.BlockSpec((1, QT, GW), lambda b, g, i: (b, i, g)),
        out_shape=jax.ShapeDtypeStruct((batch, seq, D_MODEL), BF16),
        compiler_params=_cparams(("parallel", "parallel", "parallel")),
        name="win_attn",
    )(z3, g3, kwt, vwaug)


def _pool_kernel(v_ref, prev_ref, gate_ref, w_ref, scale_ref, o_ref, *, tt):
    i = pl.program_id(1)
    cur = v_ref[0].astype(F32)
    prev = jnp.where(i > 0, prev_ref[0].astype(F32), 0.0)
    t = i * tt + lax.broadcasted_iota(jnp.int32, (tt, 1), 0)
    for g, w in enumerate(POOL_WINDOWS):
        sl = slice(g * POOL_GROUP, (g + 1) * POOL_GROUP)
        ext = jnp.concatenate([prev[:, sl], cur[:, sl]], axis=0)
        acc = ext
        d = 1
        while d < w:
            acc = acc + pltpu.roll(acc, d, 0)
            d *= 2
        cnt = jnp.minimum(t + 1, w).astype(F32)
        pooled = acc[POOL_HALO:, :] / cnt - cur[:, sl]
        mixed = jnp.dot(pooled.astype(BF16), w_ref[g], preferred_element_type=F32)
        o_ref[0, :, sl] = (mixed * scale_ref[:, sl] * _silu(gate_ref[0, :, sl].astype(F32))).astype(BF16)


def _pool(z16, w_pool, scale, batch, seq, tt):
    z3 = z16.reshape(batch, seq, Z16_W)
    hb = tt // POOL_HALO
    return pl.pallas_call(
        functools.partial(_pool_kernel, tt=tt),
        grid=(batch, seq // tt),
        in_specs=[pl.BlockSpec((1, tt, D_MODEL), lambda b, i: (b, i, Z16_BV // D_MODEL)),
                  pl.BlockSpec((1, POOL_HALO, D_MODEL),
                               lambda b, i: (b, jnp.maximum(i * hb - 1, 0), Z16_BV // D_MODEL)),
                  pl.BlockSpec((1, tt, D_MODEL), lambda b, i: (b, i, Z16_BSILU // D_MODEL)),
                  pl.BlockSpec((len(POOL_WINDOWS), POOL_GROUP, POOL_GROUP), lambda b, i: (0, 0, 0)),
                  pl.BlockSpec((1, D_MODEL), lambda b, i: (0, 0))],
        out_specs=pl.BlockSpec((1, tt, D_MODEL), lambda b, i: (b, i, 0)),
        out_shape=jax.ShapeDtypeStruct((batch, seq, D_MODEL), BF16),
        compiler_params=_cparams(("parallel", "parallel")),
        name="pool",
    )(z3, z3, z3, w_pool, scale)


def _hgrn_kernel(q_ref, f_ref, i_ref, gate_ref, lb_ref, gain_ref, o_ref, st_ref, *, n_chunks):
    c, sub = HG_CHUNK, HG_SUB

    @pl.when(pl.program_id(2) == 0)
    def _():
        st_ref[...] = jnp.zeros(st_ref.shape, F32)

    lb = lb_ref[0]
    gain = gain_ref[0]
    row = lax.broadcasted_iota(jnp.int32, (c, 1), 0)
    tri = jnp.where(lax.broadcasted_iota(jnp.int32, (c, c), 1) <= row, 1.0, 0.0).astype(BF16)
    col_c = lax.broadcasted_iota(jnp.int32, (1, c), 1)
    same_half = (row // (2 * sub)) == (col_c // (2 * sub))
    sub_row = lax.broadcasted_iota(jnp.int32, (sub, 1), 0)

    def chunk(n, carry):
        r0 = pl.multiple_of(n * c, c)
        f = lb + (1.0 - lb) * _sigmoid(f_ref[0, pl.ds(r0, c), :])
        lf = jnp.log(f)
        k = 1.0 - f
        q = q_ref[0, pl.ds(r0, c), :].astype(F32)
        v = i_ref[0, pl.ds(r0, c), :]
        hi = lf.astype(BF16)
        r1 = lf - hi.astype(F32)
        mid = r1.astype(BF16)
        lo = (r1 - mid.astype(F32)).astype(BF16)
        a = (jnp.dot(tri, hi, preferred_element_type=F32)
             + jnp.dot(tri, mid, preferred_element_type=F32)
             + jnp.dot(tri, lo, preferred_element_type=F32))

        st = st_ref[...]
        o = lax.dot_general((q * jnp.exp(a)).astype(BF16), st.astype(BF16),
                            (((1,), (1,)), ((), ())), preferred_element_type=F32)

        ref32 = a[2 * sub - 1:2 * sub, :]
        qt = jnp.where(row >= 2 * sub, q * jnp.exp(jnp.minimum(a - ref32, 0.0)), 0.0)
        kt = jnp.where(row < 2 * sub, k * jnp.exp(jnp.minimum(ref32 - a, 0.0)), 0.0)
        att = lax.dot_general(qt.astype(BF16), kt.astype(BF16), (((1,), (1,)), ((), ())),
                              preferred_element_type=F32)
        ref16 = jnp.where(row < 2 * sub, a[sub - 1:sub, :], a[3 * sub - 1:3 * sub, :])
        right = (row % (2 * sub)) >= sub
        qt = jnp.where(right, q * jnp.exp(jnp.minimum(a - ref16, 0.0)), 0.0)
        kt = jnp.where(right, 0.0, k * jnp.exp(jnp.minimum(ref16 - a, 0.0)))
        att16 = lax.dot_general(qt.astype(BF16), kt.astype(BF16), (((1,), (1,)), ((), ())),
                                preferred_element_type=F32)
        att = att + jnp.where(same_half, att16, 0.0)
        diag = []
        for b in range(c // sub):
            ab = a[b * sub:(b + 1) * sub, :]
            qb = q[b * sub:(b + 1) * sub, :]
            kb = k[b * sub:(b + 1) * sub, :]
            blk = jnp.zeros((sub, c), F32)
            for s in range(sub):
                e = jnp.exp(jnp.minimum(ab - ab[s:s + 1, :], 0.0))
                colv = jnp.sum(qb * kb[s:s + 1, :] * e, axis=1, keepdims=True)
                colv = jnp.where(sub_row >= s, colv, 0.0)
                blk = blk + jnp.where(col_c == b * sub + s, colv, 0.0)
            diag.append(blk)
        att = att + jnp.concatenate(diag, axis=0)
        o = o + jnp.dot(att.astype(BF16), v, preferred_element_type=F32)

        a_last = a[c - 1:c, :]
        kd = (k * jnp.exp(a_last - a)).astype(BF16)
        st_ref[...] = st * jnp.exp(a_last) + lax.dot_general(
            v, kd, (((0,), (0,)), ((), ())), preferred_element_type=F32)

        ms = jnp.mean(o * o, axis=-1, keepdims=True)
        y = o * lax.rsqrt(ms + NORM_EPS) * gain
        o_ref[0, pl.ds(r0, c), :] = (y * _silu(gate_ref[0, pl.ds(r0, c), :].astype(F32))).astype(BF16)
        return carry

    lax.fori_loop(0, n_chunks, chunk, 0)


def _hgrn(z16, z32, lb, gain, batch, seq, tc):
    z3 = z16.reshape(batch, seq, Z16_W)
    f3 = z32.reshape(batch, seq, Z32_W)

    def col16(off):
        return pl.BlockSpec((1, tc, HG_DK), lambda b, h, i: (b, i, off // HG_DK + h))

    return pl.pallas_call(
        functools.partial(_hgrn_kernel, n_chunks=tc // HG_CHUNK),
        grid=(batch, HG_HEADS, seq // tc),
        in_specs=[col16(Z16_CQ),
                  pl.BlockSpec((1, tc, HG_DK), lambda b, h, i: (b, i, Z32_CF // HG_DK + h)),
                  col16(Z16_CI), col16(Z16_CSILU),
                  pl.BlockSpec((1, 1, HG_DK), lambda b, h, i: (h, 0, 0)),
                  pl.BlockSpec((1, 1, HG_DK), lambda b, h, i: (h, 0, 0))],
        out_specs=pl.BlockSpec((1, tc, HG_DK), lambda b, h, i: (b, i, h)),
        out_shape=jax.ShapeDtypeStruct((batch, seq, D_MODEL), BF16),
        scratch_shapes=[pltpu.VMEM((HG_DK, HG_DK), F32)],
        compiler_params=_cparams(("parallel", "parallel", "arbitrary")),
        name="hgrn",
    )(z3, f3, z3, z3, lb, gain)


def _outproj_kernel(oc_ref, os_ref, ow_ref, asilu_ref, ob_ref, ocn_ref, m0_ref, m1_ref, m2_ref,
                    x_ref, wb_ref, wo_ref, post_ref, o_ref):
    oa = oc_ref[...].astype(F32) + os_ref[...].astype(F32) + ow_ref[...].astype(F32)
    oa = (oa * _silu(asilu_ref[...].astype(F32))).astype(BF16)
    merged = _sigmoid(m0_ref[...].astype(F32)) * jnp.dot(oa, wb_ref[0], preferred_element_type=F32)
    merged += _sigmoid(m1_ref[...].astype(F32)) * jnp.dot(ob_ref[...], wb_ref[1], preferred_element_type=F32)
    merged += _sigmoid(m2_ref[...].astype(F32)) * jnp.dot(ocn_ref[...], wb_ref[2], preferred_element_type=F32)
    out = jnp.dot(merged.astype(BF16), wo_ref[...], preferred_element_type=F32)
    ms = jnp.mean(out * out, axis=-1, keepdims=True)
    o_ref[...] = x_ref[...] + out * lax.rsqrt(ms + NORM_EPS) * post_ref[...]


def _outproj(oc, osel, ow, z16, ob, ocn, x2, wb, wo, post, tm):
    m = x2.shape[0]

    def rows(j=0):
        return pl.BlockSpec((tm, D_MODEL), lambda i: (i, j))

    return pl.pallas_call(
        _outproj_kernel,
        grid=(m // tm,),
        in_specs=[rows(), rows(), rows(), rows(Z16_ASILU // D_MODEL), rows(), rows(),
                  rows(Z16_MERGE // D_MODEL), rows(Z16_MERGE // D_MODEL + 1), rows(Z16_MERGE // D_MODEL + 2),
                  rows(),
                  pl.BlockSpec((3, D_MODEL, D_MODEL), lambda i: (0, 0, 0)),
                  pl.BlockSpec((D_MODEL, D_MODEL), lambda i: (0, 0)),
                  pl.BlockSpec((1, D_MODEL), lambda i: (0, 0))],
        out_specs=rows(),
        out_shape=jax.ShapeDtypeStruct((m, D_MODEL), F32),
        compiler_params=_cparams(("parallel",)),
        name="outproj",
    )(oc, osel, ow, z16, ob, ocn, z16, z16, z16, x2, wb, wo, post)


def _split_w_in(w):
    sizes = (1024, 256, 256, 256, 256, 256, 256, 3 * N_G * HPG, 1024, 1024, 1024, 1024, 1024, 1024, 1024, 3072)
    names = ("a_q", "a_kc", "a_vc", "a_ks", "a_vs", "a_kw", "a_vw", "a_gate", "a_silu", "b_v", "b_silu",
             "c_q", "c_f", "c_i", "c_silu", "merge")
    parts, off = {}, 0
    for n, s in zip(names, sizes):
        parts[n] = w[:, off:off + s]
        off += s
    w16 = jnp.concatenate([parts[n] for n in ("a_q", "a_ks", "a_vs", "a_kw", "a_vw", "a_silu", "b_v", "b_silu",
                                              "c_q", "c_i", "c_silu", "merge")], axis=1)
    gate = parts["a_gate"].reshape(D_MODEL, N_G, 3 * HPG)
    gate = jnp.pad(gate, ((0, 0), (0, 0), (0, GATE_PAD - 3 * HPG))).reshape(D_MODEL, N_G * GATE_PAD)
    w32 = jnp.concatenate([parts["c_f"], parts["a_kc"], parts["a_vc"], gate], axis=1)
    return w16.astype(BF16), w32.astype(BF16)


def _block_diag_heads(w):
    n = CMP_LANES // DH
    eye = jnp.eye(n, dtype=w.dtype)
    return jnp.einsum("gh,pde->pgdhe", eye, w).reshape(CMP_BLOCK, CMP_LANES, CMP_LANES)


def _overlap_matrix(seq):
    n_c = seq // CMP_STRIDE
    n_s = seq // SEL_BLOCK
    c_start = jnp.arange(n_c) * CMP_STRIDE
    s_start = jnp.arange(n_s) * SEL_BLOCK
    ov = jnp.clip(jnp.minimum(c_start[None, :] + CMP_BLOCK, s_start[:, None] + SEL_BLOCK)
                  - jnp.maximum(c_start[None, :], s_start[:, None]), 0).astype(F32) / CMP_BLOCK
    return ov.astype(BF16)


def kernel(x, pre_norm, w_in, cmp_pe_k, cmp_pe_v, cmp_w_k, cmp_w_v, pool_w, pool_scale, hgrn_lb_logits, hgrn_norm,
           w_branch, w_out, post_norm):
    batch, seq, _ = x.shape
    depth = w_in.shape[0]
    m = batch * seq
    probs = jax.nn.softmax(hgrn_lb_logits.astype(F32), axis=0)
    lbs = jnp.cumsum(probs, axis=0) - probs[0]
    ov = _overlap_matrix(seq)
    tm_in = min(1024, m)
    tt = min(512, seq)

    x2 = x.reshape(m, D_MODEL)
    for l in range(depth):
        w16, w32 = _split_w_in(w_in[l])
        gain = pre_norm[l].reshape(1, D_MODEL)
        z16 = _inproj(x2, gain, w16, BF16, tm_in, 512)
        z32 = _inproj(x2, gain, w32, F32, tm_in, 512)

        pe = jnp.stack([jnp.tile(cmp_pe_k[l], (1, CMP_LANES // DH)), jnp.tile(cmp_pe_v[l], (1, CMP_LANES // DH))])
        wbd = jnp.stack([_block_diag_heads(cmp_w_k[l]), _block_diag_heads(cmp_w_v[l])]).astype(BF16)
        ckv, ckvt = _compress(z32, pe, wbd, batch, seq)
        oc, sbias = _cmp_select(z16, z32, ckv, ckvt, ov, batch, seq)
        kaug, vaug, kwt, vwaug = _kv_prep(z16, batch, seq, tt)
        osel = _sel_attn(z16, z32, sbias, kaug, vaug, batch, seq, tt)
        ow = _win_attn(z16, z32, kwt, vwaug, batch, seq)

        ob = _pool(z16, pool_w[l].astype(BF16), pool_scale[l].reshape(1, D_MODEL), batch, seq, tt)
        ocn = _hgrn(z16, z32, lbs[l].reshape(HG_HEADS, 1, HG_DK), hgrn_norm[l].reshape(HG_HEADS, 1, HG_DK),
                    batch, seq, tt)

        x2 = _outproj(oc.reshape(m, D_MODEL), osel.reshape(m, D_MODEL), ow.reshape(m, D_MODEL), z16,
                      ob.reshape(m, D_MODEL), ocn.reshape(m, D_MODEL), x2,
                      w_branch[l].astype(BF16), w_out[l].astype(BF16), post_norm[l].reshape(1, D_MODEL), 256)
    return x2.reshape(batch, seq, D_MODEL)
```

```python
import functools

import jax
import jax.numpy as jnp
from jax import lax
from jax.experimental import pallas as pl
from jax.experimental.pallas import tpu as pltpu

F32 = jnp.float32
BF16 = jnp.bfloat16

D_MODEL = 1024
NORM_EPS = 1e-6
DH = 64
N_G = 4
HPG = 4
GW = HPG * DH
KVW = N_G * DH
CMP_BLOCK = 32
CMP_STRIDE = 16
CMP_LANES = 128
SEL_BLOCK = 64
N_SELECT = 16
WINDOW = 512
QT = 128
FORCE_SCORE = 1.0e4
ATT_SCALE = DH ** -0.5
MASK_BIAS = -(2.0 ** 30)
POOL_WINDOWS = (2, 4, 8, 16)
POOL_GROUP = D_MODEL // len(POOL_WINDOWS)
POOL_HALO = 16
HG_HEADS = 8
HG_DK = 128
HG_CHUNK = 64
HG_SUB = 16

Z16_Q, Z16_KS, Z16_VS, Z16_KW, Z16_VW = 0, 1024, 1280, 1536, 1792
Z16_ASILU, Z16_BV, Z16_BSILU = 2048, 3072, 4096
Z16_CQ, Z16_CI, Z16_CSILU, Z16_MERGE = 5120, 6144, 7168, 8192
Z16_W = 11264
Z32_CF, Z32_KC, Z32_VC, Z32_GATE = 0, 1024, 1280, 1536
Z32_W = 2048
GATE_PAD = 128

VMEM_LIMIT = 56 * 1024 * 1024


def _cparams(sem):
    return pltpu.CompilerParams(dimension_semantics=sem, vmem_limit_bytes=VMEM_LIMIT)


def _sigmoid(x):
    return 1.0 / (1.0 + jnp.exp(-x))


def _silu(x):
    return x * _sigmoid(x)


def _inproj_kernel(x_ref, g_ref, w_ref, o_ref, h_ref):
    @pl.when(pl.program_id(1) == 0)
    def _():
        x = x_ref[...]
        ms = jnp.mean(x * x, axis=-1, keepdims=True)
        h_ref[...] = (x * lax.rsqrt(ms + NORM_EPS) * g_ref[...]).astype(BF16)

    o_ref[...] = jnp.dot(h_ref[...], w_ref[...], preferred_element_type=F32).astype(o_ref.dtype)


def _inproj(x2, gain, w, out_dtype, tm, tn):
    m, d = x2.shape
    n = w.shape[1]
    return pl.pallas_call(
        _inproj_kernel,
        grid=(m // tm, n // tn),
        in_specs=[pl.BlockSpec((tm, d), lambda i, j: (i, 0)),
                  pl.BlockSpec((1, d), lambda i, j: (0, 0)),
                  pl.BlockSpec((d, tn), lambda i, j: (0, j))],
        out_specs=pl.BlockSpec((tm, tn), lambda i, j: (i, j)),
        out_shape=jax.ShapeDtypeStruct((m, n), out_dtype),
        scratch_shapes=[pltpu.VMEM((tm, d), BF16)],
        compiler_params=_cparams(("parallel", "arbitrary")),
        name="inproj",
    )(x2, gain, w)


def _compress_kernel(x_ref, pe_ref, w_ref, o_ref, ot_ref, *, nch):
    acc_a = jnp.zeros((nch, CMP_LANES), F32)
    acc_b = jnp.zeros((nch, CMP_LANES), F32)
    for p in range(CMP_STRIDE):
        xp = x_ref[0, pl.ds(p, nch, stride=CMP_STRIDE), :]
        a = (xp + pe_ref[0, p:p + 1, :]).astype(BF16)
        b = (xp + pe_ref[0, CMP_STRIDE + p:CMP_STRIDE + p + 1, :]).astype(BF16)
        acc_a += jnp.dot(a, w_ref[0, p], preferred_element_type=F32)
        acc_b += jnp.dot(b, w_ref[0, CMP_STRIDE + p], preferred_element_type=F32)
    c = acc_a + pltpu.roll(acc_b, nch - 1, 0)
    ct = c.T
    for g in range(CMP_LANES // DH):
        o_ref[0, 0, g] = c[:, g * DH:(g + 1) * DH].astype(BF16)
        ot_ref[0, 0, g] = ct[g * DH:(g + 1) * DH, :].astype(BF16)


def _compress(z32, pe, wbd, batch, seq):
    nch = seq // CMP_STRIDE
    z3 = z32.reshape(batch, seq, Z32_W)
    gpb = CMP_LANES // DH
    return pl.pallas_call(
        functools.partial(_compress_kernel, nch=nch),
        grid=(batch, 2, N_G // gpb),
        in_specs=[pl.BlockSpec((1, seq, CMP_LANES),
                               lambda b, s, c: (b, 0, (Z32_KC + s * KVW) // CMP_LANES + c)),
                  pl.BlockSpec((1, CMP_BLOCK, CMP_LANES), lambda b, s, c: (s, 0, 0)),
                  pl.BlockSpec((1, CMP_BLOCK, CMP_LANES, CMP_LANES), lambda b, s, c: (s, 0, 0, 0))],
        out_specs=[pl.BlockSpec((1, 1, gpb, nch, DH), lambda b, s, c: (b, s, c, 0, 0)),
                   pl.BlockSpec((1, 1, gpb, DH, nch), lambda b, s, c: (b, s, c, 0, 0))],
        out_shape=[jax.ShapeDtypeStruct((batch, 2, N_G, nch, DH), BF16),
                   jax.ShapeDtypeStruct((batch, 2, N_G, DH, nch), BF16)],
        compiler_params=_cparams(("parallel", "parallel", "parallel")),
        name="compress",
    )(z3, pe, wbd)


def _cmp_select_kernel(q_ref, ck_ref, cvt_ref, ov_ref, gate_ref, oc_ref, qa_ref, *, nch, n_s, n_sel):
    s0 = pl.program_id(2) * QT
    qt = q_ref[0].astype(F32).T
    ck = ck_ref[0, 0, 0]
    cvt = cvt_ref[0, 0, 0]
    t_lane = s0 + lax.broadcasted_iota(jnp.int32, (1, QT), 1)
    c_end = lax.broadcasted_iota(jnp.int32, (nch, 1), 0) * CMP_STRIDE + (CMP_BLOCK - 1)
    visible = c_end <= t_lane
    gates = _sigmoid(gate_ref[0])

    p_sum = jnp.zeros((nch, QT), F32)
    outs = []
    for h in range(HPG):
        qh = qt[h * DH:(h + 1) * DH, :].astype(BF16)
        s = jnp.dot(ck, qh, preferred_element_type=F32) * ATT_SCALE
        s = jnp.where(visible, s, -jnp.inf)
        m = jnp.max(s, axis=0, keepdims=True)
        m = jnp.where(m == -jnp.inf, 0.0, m)
        e = jnp.exp(s - m)
        p = e / jnp.maximum(jnp.sum(e, axis=0, keepdims=True), 1e-30)
        p_sum += p
        oct_h = jnp.dot(cvt, p.astype(BF16), preferred_element_type=F32)
        outs.append(oct_h.T * gates[:, 3 * h:3 * h + 1])
    oc_ref[0] = jnp.concatenate(outs, axis=1).astype(BF16)

    p_hi = p_sum.astype(BF16)
    p_lo = (p_sum - p_hi.astype(F32)).astype(BF16)
    ov = ov_ref[...]
    imp = (jnp.dot(ov, p_hi, preferred_element_type=F32)
           + jnp.dot(ov, p_lo, preferred_element_type=F32))

    blk = lax.broadcasted_iota(jnp.int32, (n_s, 1), 0)
    cur = t_lane // SEL_BLOCK
    valid = blk <= cur
    force = (blk == 0) | (blk == cur) | (blk == cur - 1)
    score = jnp.where(valid, jnp.where(force, FORCE_SCORE, imp), -1.0)

    def pick(_, carry):
        sc, sel = carry
        m = jnp.max(sc, axis=0, keepdims=True)
        first = jnp.min(jnp.where(sc == m, blk, n_s), axis=0, keepdims=True)
        hit = blk == first
        return jnp.where(hit, -jnp.inf, sc), jnp.where(hit, 1.0, sel)

    _, sel = lax.fori_loop(0, n_sel, pick, (score, jnp.zeros((n_s, QT), F32)), unroll=True)
    bias_t = jnp.where((sel > 0.5) & valid, 0.0, MASK_BIAS).astype(BF16)
    qa_ref[0, 0, 0, 0:n_s, :] = jnp.concatenate([bias_t] * HPG, axis=1)
    qa_ref[0, 0, 0, n_s:n_s + DH, :] = jnp.concatenate(
        [(qt[h * DH:(h + 1) * DH, :] * ATT_SCALE).astype(BF16) for h in range(HPG)], axis=1)


def _cmp_select(z16, z32, ck, cvt, ov, batch, seq):
    nch = seq // CMP_STRIDE
    n_s = seq // SEL_BLOCK
    n_sel = min(N_SELECT, n_s)
    z3 = z16.reshape(batch, seq, Z16_W)
    g3 = z32.reshape(batch, seq, Z32_W)
    return pl.pallas_call(
        functools.partial(_cmp_select_kernel, nch=nch, n_s=n_s, n_sel=n_sel),
        grid=(batch, N_G, seq // QT),
        in_specs=[pl.BlockSpec((1, QT, GW), lambda b, g, i: (b, i, Z16_Q // GW + g)),
                  pl.BlockSpec((1, 1, 1, nch, DH), lambda b, g, i: (b, 0, g, 0, 0)),
                  pl.BlockSpec((1, 1, 1, DH, nch), lambda b, g, i: (b, 1, g, 0, 0)),
                  pl.BlockSpec((n_s, nch), lambda b, g, i: (0, 0)),
                  pl.BlockSpec((1, QT, GATE_PAD), lambda b, g, i: (b, i, Z32_GATE // GATE_PAD + g))],
        out_specs=[pl.BlockSpec((1, QT, GW), lambda b, g, i: (b, i, g)),
                   pl.BlockSpec((1, 1, 1, n_s + DH, HPG * QT), lambda b, g, i: (b, g, i, 0, 0))],
        out_shape=[jax.ShapeDtypeStruct((batch, seq, D_MODEL), BF16),
                   jax.ShapeDtypeStruct((batch, N_G, seq // QT, n_s + DH, HPG * QT), BF16)],
        compiler_params=_cparams(("parallel", "parallel", "parallel")),
        name="cmp_select",
    )(z3, ck, cvt, ov, g3)


def _kv_prep_kernel(ks_ref, vs_ref, kw_ref, vw_ref, ka_ref, va_ref, kwt_ref, vwa_ref, *, n_s, tt):
    t0 = pl.program_id(1) * tt
    ks = ks_ref[0]
    vst = vs_ref[0].astype(F32).T
    kwt = kw_ref[0].astype(F32).T
    vw = vw_ref[0]
    pos = t0 + lax.broadcasted_iota(jnp.int32, (tt, 1), 0)
    blk = lax.broadcasted_iota(jnp.int32, (1, n_s), 1)
    onehot = jnp.where(pos // SEL_BLOCK == blk, 1.0, 0.0).astype(BF16)
    ones_row = jnp.where(lax.broadcasted_iota(jnp.int32, (DH, tt), 0) == 0, 1.0, 0.0).astype(BF16)
    ones_col = jnp.where(lax.broadcasted_iota(jnp.int32, (tt, DH), 1) == 0, 1.0, 0.0).astype(BF16)
    for g in range(N_G):
        ka_ref[0, g, :, 0:n_s] = onehot
        ka_ref[0, g, :, n_s:n_s + DH] = ks[:, g * DH:(g + 1) * DH]
        va_ref[0, g, 0:DH, :] = vst[g * DH:(g + 1) * DH, :].astype(BF16)
        va_ref[0, g, DH:2 * DH, :] = ones_row
        kwt_ref[0, g] = kwt[g * DH:(g + 1) * DH, :].astype(BF16)
        vwa_ref[0, g] = jnp.concatenate([vw[:, g * DH:(g + 1) * DH], ones_col], axis=1)


def _kv_prep(z16, batch, seq, tt):
    n_s = seq // SEL_BLOCK
    z3 = z16.reshape(batch, seq, Z16_W)

    def col(off):
        return pl.BlockSpec((1, tt, KVW), lambda b, i: (b, i, off // KVW))

    return pl.pallas_call(
        functools.partial(_kv_prep_kernel, n_s=n_s, tt=tt),
        grid=(batch, seq // tt),
        in_specs=[col(Z16_KS), col(Z16_VS), col(Z16_KW), col(Z16_VW)],
        out_specs=[pl.BlockSpec((1, N_G, tt, n_s + DH), lambda b, i: (b, 0, i, 0)),
                   pl.BlockSpec((1, N_G, 2 * DH, tt), lambda b, i: (b, 0, 0, i)),
                   pl.BlockSpec((1, N_G, DH, tt), lambda b, i: (b, 0, 0, i)),
                   pl.BlockSpec((1, N_G, tt, 2 * DH), lambda b, i: (b, 0, i, 0))],
        out_shape=[jax.ShapeDtypeStruct((batch, N_G, seq, n_s + DH), BF16),
                   jax.ShapeDtypeStruct((batch, N_G, 2 * DH, seq), BF16),
                   jax.ShapeDtypeStruct((batch, N_G, DH, seq), BF16),
                   jax.ShapeDtypeStruct((batch, N_G, seq, 2 * DH), BF16)],
        compiler_params=_cparams(("parallel", "parallel")),
        name="kv_prep",
    )(z3, z3, z3, z3)


def _sel_attn_kernel(qa_ref, gate_ref, k_ref, v_ref, o_ref, sa_ref, sb_ref, m_ref, acc_ref, *, tk):
    s0 = pl.program_id(2) * QT
    unit = 2 * tk
    n_units = s0 // unit
    u0 = pl.multiple_of(n_units * unit, unit)
    m_ref[...] = jnp.full(m_ref.shape, -jnp.inf, F32)
    acc_ref[...] = jnp.zeros(acc_ref.shape, F32)
    t_lane = s0 + lax.broadcasted_iota(jnp.int32, (1, HPG * QT), 1) % QT

    def scores(off):
        return jnp.dot(k_ref[0, 0, pl.ds(off, tk), :], qa_ref[0, 0, 0], preferred_element_type=F32)

    def causal(s, off):
        kpos = off + lax.broadcasted_iota(jnp.int32, (tk, 1), 0)
        return jnp.where(kpos <= t_lane, s, MASK_BIAS)

    def accumulate(s, off):
        m_prev = m_ref[...]
        m_new = jnp.maximum(m_prev, jnp.max(s, axis=0, keepdims=True))
        p = jnp.exp(s - m_new).astype(BF16)
        acc_ref[...] = jnp.exp(m_prev - m_new) * acc_ref[...] + jnp.dot(
            v_ref[0, 0, :, pl.ds(off, tk)], p, preferred_element_type=F32)
        m_ref[...] = m_new

    accumulate(causal(scores(u0), u0), u0)

    @pl.when(s0 >= u0 + tk)
    def _():
        off = pl.multiple_of(u0 + tk, tk)
        accumulate(causal(scores(off), off), off)

    @pl.when(n_units > 0)
    def _():
        sa_ref[...] = scores(0)

        def body(u, carry):
            off = pl.multiple_of(u * unit, unit)
            sb_ref[...] = scores(off + tk)
            accumulate(sa_ref[...], off)
            nxt = pl.multiple_of(jnp.minimum(u + 1, n_units - 1) * unit, unit)
            sa_ref[...] = scores(nxt)
            accumulate(sb_ref[...], off + tk)
            return carry

        lax.fori_loop(0, n_units, body, 0)

    gates = _sigmoid(gate_ref[0])
    outs = []
    for h in range(HPG):
        a = acc_ref[:, h * QT:(h + 1) * QT]
        o_t = a[0:DH, :] / a[DH:DH + 1, :]
        outs.append(o_t.T * gates[:, 3 * h + 1:3 * h + 2])
    o_ref[0] = jnp.concatenate(outs, axis=1).astype(BF16)


def _sel_attn(qaug, z32, kaug, vaug, batch, seq, tk):
    kd = kaug.shape[-1]
    g3 = z32.reshape(batch, seq, Z32_W)
    return pl.pallas_call(
        functools.partial(_sel_attn_kernel, tk=tk),
        grid=(batch, N_G, seq // QT),
        in_specs=[pl.BlockSpec((1, 1, 1, kd, HPG * QT), lambda b, g, i: (b, g, i, 0, 0)),
                  pl.BlockSpec((1, QT, GATE_PAD), lambda b, g, i: (b, i, Z32_GATE // GATE_PAD + g)),
                  pl.BlockSpec((1, 1, seq, kd), lambda b, g, i: (b, g, 0, 0)),
                  pl.BlockSpec((1, 1, 2 * DH, seq), lambda b, g, i: (b, g, 0, 0))],
        out_specs=pl.BlockSpec((1, QT, GW), lambda b, g, i: (b, i, g)),
        out_shape=jax.ShapeDtypeStruct((batch, seq, D_MODEL), BF16),
        scratch_shapes=[pltpu.VMEM((tk, HPG * QT), F32),
                        pltpu.VMEM((tk, HPG * QT), F32),
                        pltpu.VMEM((1, HPG * QT), F32),
                        pltpu.VMEM((2 * DH, HPG * QT), F32)],
        compiler_params=_cparams(("parallel", "parallel", "arbitrary")),
        name="sel_attn",
    )(qaug, g3, kaug, vaug)


def _win_attn_kernel(q_ref, gate_ref, k_ref, v_ref, o_ref, *, span):
    s0 = pl.program_id(2) * QT
    start = pl.multiple_of(jnp.maximum(s0 - WINDOW, 0), QT)
    q = q_ref[0]
    qs = jnp.concatenate(
        [(q[:, h * DH:(h + 1) * DH].astype(F32) * ATT_SCALE).astype(BF16) for h in range(HPG)], axis=0)
    s = jnp.dot(qs, k_ref[0, 0, :, pl.ds(start, span)], preferred_element_type=F32)
    kpos = start + lax.broadcasted_iota(jnp.int32, (1, span), 1)
    t = s0 + lax.broadcasted_iota(jnp.int32, (HPG * QT, 1), 0) % QT
    s = jnp.where((kpos <= t) & (t - kpos < WINDOW), s, -jnp.inf)
    e = jnp.exp(s - jnp.max(s, axis=1, keepdims=True))
    a = jnp.dot(e.astype(BF16), v_ref[0, 0, pl.ds(start, span), :], preferred_element_type=F32)
    gates = _sigmoid(gate_ref[0])
    outs = []
    for h in range(HPG):
        ah = a[h * QT:(h + 1) * QT, :]
        outs.append(ah[:, 0:DH] / ah[:, DH:DH + 1] * gates[:, 3 * h + 2:3 * h + 3])
    o_ref[0] = jnp.concatenate(outs, axis=1).astype(BF16)


def _win_attn(z16, z32, kwt, vwaug, batch, seq):
    span = min(WINDOW + QT, seq)
    z3 = z16.reshape(batch, seq, Z16_W)
    g3 = z32.reshape(batch, seq, Z32_W)
    return pl.pallas_call(
        functools.partial(_win_attn_kernel, span=span),
        grid=(batch, N_G, seq // QT),
        in_specs=[pl.BlockSpec((1, QT, GW), lambda b, g, i: (b, i, Z16_Q // GW + g)),
                  pl.BlockSpec((1, QT, GATE_PAD), lambda b, g, i: (b, i, Z32_GATE // GATE_PAD + g)),
                  pl.BlockSpec((1, 1, DH, seq), lambda b, g, i: (b, g, 0, 0)),
                  pl.BlockSpec((1, 1, seq, 2 * DH), lambda b, g, i: (b, g, 0, 0))],
        out_specs=pl.BlockSpec((1, QT, GW), lambda b, g, i: (b, i, g)),
        out_shape=jax.ShapeDtypeStruct((batch, seq, D_MODEL), BF16),
        compiler_params=_cparams(("parallel", "parallel", "parallel")),
        name="win_attn",
    )(z3, g3, kwt, vwaug)


def _pool_kernel(v_ref, prev_ref, gate_ref, w_ref, scale_ref, o_ref, *, tt):
    i = pl.program_id(1)
    cur = v_ref[0].astype(F32)
    prev = jnp.where(i > 0, prev_ref[0].astype(F32), 0.0)
    t = i * tt + lax.broadcasted_iota(jnp.int32, (tt, 1), 0)
    for g, w in enumerate(POOL_WINDOWS):
        sl = slice(g * POOL_GROUP, (g + 1) * POOL_GROUP)
        ext = jnp.concatenate([prev[:, sl], cur[:, sl]], axis=0)
        acc = ext
        d = 1
        while d < w:
            acc = acc + pltpu.roll(acc, d, 0)
            d *= 2
        cnt = jnp.minimum(t + 1, w).astype(F32)
        pooled = acc[POOL_HALO:, :] / cnt - cur[:, sl]
        mixed = jnp.dot(pooled.astype(BF16), w_ref[g], preferred_element_type=F32)
        o_ref[0, :, sl] = (mixed * scale_ref[:, sl] * _silu(gate_ref[0, :, sl].astype(F32))).astype(BF16)


def _pool(z16, w_pool, scale, batch, seq, tt):
    z3 = z16.reshape(batch, seq, Z16_W)
    hb = tt // POOL_HALO
    return pl.pallas_call(
        functools.partial(_pool_kernel, tt=tt),
        grid=(batch, seq // tt),
        in_specs=[pl.BlockSpec((1, tt, D_MODEL), lambda b, i: (b, i, Z16_BV // D_MODEL)),
                  pl.BlockSpec((1, POOL_HALO, D_MODEL),
                               lambda b, i: (b, jnp.maximum(i * hb - 1, 0), Z16_BV // D_MODEL)),
                  pl.BlockSpec((1, tt, D_MODEL), lambda b, i: (b, i, Z16_BSILU // D_MODEL)),
                  pl.BlockSpec((len(POOL_WINDOWS), POOL_GROUP, POOL_GROUP), lambda b, i: (0, 0, 0)),
                  pl.BlockSpec((1, D_MODEL), lambda b, i: (0, 0))],
        out_specs=pl.BlockSpec((1, tt, D_MODEL), lambda b, i: (b, i, 0)),
        out_shape=jax.ShapeDtypeStruct((batch, seq, D_MODEL), BF16),
        compiler_params=_cparams(("parallel", "parallel")),
        name="pool",
    )(z3, z3, z3, w_pool, scale)


def _hgrn_kernel(q_ref, f_ref, i_ref, gate_ref, lb_ref, gain_ref, o_ref, st_ref, *, n_chunks):
    c, sub = HG_CHUNK, HG_SUB

    @pl.when(pl.program_id(2) == 0)
    def _():
        st_ref[...] = jnp.zeros(st_ref.shape, F32)

    lb = lb_ref[0]
    gain = gain_ref[0]
    row = lax.broadcasted_iota(jnp.int32, (c, 1), 0)
    tri = jnp.where(lax.broadcasted_iota(jnp.int32, (c, c), 1) <= row, 1.0, 0.0).astype(BF16)
    col_c = lax.broadcasted_iota(jnp.int32, (1, c), 1)
    same_half = (row // (2 * sub)) == (col_c // (2 * sub))
    sub_row = lax.broadcasted_iota(jnp.int32, (sub, 1), 0)

    def chunk(n, carry):
        r0 = pl.multiple_of(n * c, c)
        f = lb + (1.0 - lb) * _sigmoid(f_ref[0, pl.ds(r0, c), :])
        lf = jnp.log(f)
        k = 1.0 - f
        q = q_ref[0, pl.ds(r0, c), :].astype(F32)
        v = i_ref[0, pl.ds(r0, c), :]
        hi = lf.astype(BF16)
        r1 = lf - hi.astype(F32)
        mid = r1.astype(BF16)
        lo = (r1 - mid.astype(F32)).astype(BF16)
        a = (jnp.dot(tri, hi, preferred_element_type=F32)
             + jnp.dot(tri, mid, preferred_element_type=F32)
             + jnp.dot(tri, lo, preferred_element_type=F32))

        st = st_ref[...]
        o = lax.dot_general((q * jnp.exp(a)).astype(BF16), st.astype(BF16),
                            (((1,), (1,)), ((), ())), preferred_element_type=F32)

        ref32 = a[2 * sub - 1:2 * sub, :]
        qt = jnp.where(row >= 2 * sub, q * jnp.exp(jnp.minimum(a - ref32, 0.0)), 0.0)
        kt = jnp.where(row < 2 * sub, k * jnp.exp(jnp.minimum(ref32 - a, 0.0)), 0.0)
        att = lax.dot_general(qt.astype(BF16), kt.astype(BF16), (((1,), (1,)), ((), ())),
                              preferred_element_type=F32)
        ref16 = jnp.where(row < 2 * sub, a[sub - 1:sub, :], a[3 * sub - 1:3 * sub, :])
        right = (row % (2 * sub)) >= sub
        qt = jnp.where(right, q * jnp.exp(jnp.minimum(a - ref16, 0.0)), 0.0)
        kt = jnp.where(right, 0.0, k * jnp.exp(jnp.minimum(ref16 - a, 0.0)))
        att16 = lax.dot_general(qt.astype(BF16), kt.astype(BF16), (((1,), (1,)), ((), ())),
                                preferred_element_type=F32)
        att = att + jnp.where(same_half, att16, 0.0)
        diag = []
        for b in range(c // sub):
            ab = a[b * sub:(b + 1) * sub, :]
            qb = q[b * sub:(b + 1) * sub, :]
            kb = k[b * sub:(b + 1) * sub, :]
            blk = jnp.zeros((sub, c), F32)
            for s in range(sub):
                e = jnp.exp(jnp.minimum(ab - ab[s:s + 1, :], 0.0))
                colv = jnp.sum(qb * kb[s:s + 1, :] * e, axis=1, keepdims=True)
                colv = jnp.where(sub_row >= s, colv, 0.0)
                blk = blk + jnp.where(col_c == b * sub + s, colv, 0.0)
            diag.append(blk)
        att = att + jnp.concatenate(diag, axis=0)
        o = o + jnp.dot(att.astype(BF16), v, preferred_element_type=F32)

        a_last = a[c - 1:c, :]
        kd = (k * jnp.exp(a_last - a)).astype(BF16)
        st_ref[...] = st * jnp.exp(a_last) + lax.dot_general(
            v, kd, (((0,), (0,)), ((), ())), preferred_element_type=F32)

        ms = jnp.mean(o * o, axis=-1, keepdims=True)
        y = o * lax.rsqrt(ms + NORM_EPS) * gain
        o_ref[0, pl.ds(r0, c), :] = (y * _silu(gate_ref[0, pl.ds(r0, c), :].astype(F32))).astype(BF16)
        return carry

    lax.fori_loop(0, n_chunks, chunk, 0)


def _hgrn(z16, z32, lb, gain, batch, seq, tc):
    z3 = z16.reshape(batch, seq, Z16_W)
    f3 = z32.reshape(batch, seq, Z32_W)

    def col16(off):
        return pl.BlockSpec((1, tc, HG_DK), lambda b, h, i: (b, i, off // HG_DK + h))

    return pl.pallas_call(
        functools.partial(_hgrn_kernel, n_chunks=tc // HG_CHUNK),
        grid=(batch, HG_HEADS, seq // tc),
        in_specs=[col16(Z16_CQ),
                  pl.BlockSpec((1, tc, HG_DK), lambda b, h, i: (b, i, Z32_CF // HG_DK + h)),
                  col16(Z16_CI), col16(Z16_CSILU),
                  pl.BlockSpec((1, 1, HG_DK), lambda b, h, i: (h, 0, 0)),
                  pl.BlockSpec((1, 1, HG_DK), lambda b, h, i: (h, 0, 0))],
        out_specs=pl.BlockSpec((1, tc, HG_DK), lambda b, h, i: (b, i, h)),
        out_shape=jax.ShapeDtypeStruct((batch, seq, D_MODEL), BF16),
        scratch_shapes=[pltpu.VMEM((HG_DK, HG_DK), F32)],
        compiler_params=_cparams(("parallel", "parallel", "arbitrary")),
        name="hgrn",
    )(z3, f3, z3, z3, lb, gain)


def _outproj_kernel(oc_ref, os_ref, ow_ref, asilu_ref, ob_ref, ocn_ref, m0_ref, m1_ref, m2_ref,
                    x_ref, wb_ref, wo_ref, post_ref, o_ref):
    oa = oc_ref[...].astype(F32) + os_ref[...].astype(F32) + ow_ref[...].astype(F32)
    oa = (oa * _silu(asilu_ref[...].astype(F32))).astype(BF16)
    merged = _sigmoid(m0_ref[...].astype(F32)) * jnp.dot(oa, wb_ref[0], preferred_element_type=F32)
    merged += _sigmoid(m1_ref[...].astype(F32)) * jnp.dot(ob_ref[...], wb_ref[1], preferred_element_type=F32)
    merged += _sigmoid(m2_ref[...].astype(F32)) * jnp.dot(ocn_ref[...], wb_ref[2], preferred_element_type=F32)
    out = jnp.dot(merged.astype(BF16), wo_ref[...], preferred_element_type=F32)
    ms = jnp.mean(out * out, axis=-1, keepdims=True)
    o_ref[...] = x_ref[...] + out * lax.rsqrt(ms + NORM_EPS) * post_ref[...]


def _outproj(oc, osel, ow, z16, ob, ocn, x2, wb, wo, post, tm):
    m = x2.shape[0]

    def rows(j=0):
        return pl.BlockSpec((tm, D_MODEL), lambda i: (i, j))

    return pl.pallas_call(
        _outproj_kernel,
        grid=(m // tm,),
        in_specs=[rows(), rows(), rows(), rows(Z16_ASILU // D_MODEL), rows(), rows(),
                  rows(Z16_MERGE // D_MODEL), rows(Z16_MERGE // D_MODEL + 1), rows(Z16_MERGE // D_MODEL + 2),
                  rows(),
                  pl.BlockSpec((3, D_MODEL, D_MODEL), lambda i: (0, 0, 0)),
                  pl.BlockSpec((D_MODEL, D_MODEL), lambda i: (0, 0)),
                  pl.BlockSpec((1, D_MODEL), lambda i: (0, 0))],
        out_specs=rows(),
        out_shape=jax.ShapeDtypeStruct((m, D_MODEL), F32),
        compiler_params=_cparams(("parallel",)),
        name="outproj",
    )(oc, osel, ow, z16, ob, ocn, z16, z16, z16, x2, wb, wo, post)


def _split_w_in(w):
    sizes = (1024, 256, 256, 256, 256, 256, 256, 3 * N_G * HPG, 1024, 1024, 1024, 1024, 1024, 1024, 1024, 3072)
    names = ("a_q", "a_kc", "a_vc", "a_ks", "a_vs", "a_kw", "a_vw", "a_gate", "a_silu", "b_v", "b_silu",
             "c_q", "c_f", "c_i", "c_silu", "merge")
    parts, off = {}, 0
    for n, s in zip(names, sizes):
        parts[n] = w[:, off:off + s]
        off += s
    w16 = jnp.concatenate([parts[n] for n in ("a_q", "a_ks", "a_vs", "a_kw", "a_vw", "a_silu", "b_v", "b_silu",
                                              "c_q", "c_i", "c_silu", "merge")], axis=1)
    gate = parts["a_gate"].reshape(D_MODEL, N_G, 3 * HPG)
    gate = jnp.pad(gate, ((0, 0), (0, 0), (0, GATE_PAD - 3 * HPG))).reshape(D_MODEL, N_G * GATE_PAD)
    w32 = jnp.concatenate([parts["c_f"], parts["a_kc"], parts["a_vc"], gate], axis=1)
    return w16.astype(BF16), w32.astype(BF16)


def _block_diag_heads(w):
    n = CMP_LANES // DH
    eye = jnp.eye(n, dtype=w.dtype)
    return jnp.einsum("gh,pde->pgdhe", eye, w).reshape(CMP_BLOCK, CMP_LANES, CMP_LANES)


def _overlap_matrix(seq):
    n_c = seq // CMP_STRIDE
    n_s = seq // SEL_BLOCK
    c_start = jnp.arange(n_c) * CMP_STRIDE
    s_start = jnp.arange(n_s) * SEL_BLOCK
    ov = jnp.clip(jnp.minimum(c_start[None, :] + CMP_BLOCK, s_start[:, None] + SEL_BLOCK)
                  - jnp.maximum(c_start[None, :], s_start[:, None]), 0).astype(F32) / CMP_BLOCK
    return ov.astype(BF16)


def kernel(x, pre_norm, w_in, cmp_pe_k, cmp_pe_v, cmp_w_k, cmp_w_v, pool_w, pool_scale, hgrn_lb_logits, hgrn_norm,
           w_branch, w_out, post_norm):
    batch, seq, _ = x.shape
    depth = w_in.shape[0]
    m = batch * seq
    probs = jax.nn.softmax(hgrn_lb_logits.astype(F32), axis=0)
    lbs = jnp.cumsum(probs, axis=0) - probs[0]
    ov = _overlap_matrix(seq)
    tm_in = min(1024, m)
    tt = min(512, seq)

    x2 = x.reshape(m, D_MODEL)
    for l in range(depth):
        w16, w32 = _split_w_in(w_in[l])
        gain = pre_norm[l].reshape(1, D_MODEL)
        z16 = _inproj(x2, gain, w16, BF16, tm_in, 512)
        z32 = _inproj(x2, gain, w32, F32, tm_in, 512)

        pe = jnp.stack([jnp.tile(cmp_pe_k[l], (1, CMP_LANES // DH)), jnp.tile(cmp_pe_v[l], (1, CMP_LANES // DH))])
        wbd = jnp.stack([_block_diag_heads(cmp_w_k[l]), _block_diag_heads(cmp_w_v[l])]).astype(BF16)
        ckv, ckvt = _compress(z32, pe, wbd, batch, seq)
        oc, qaug = _cmp_select(z16, z32, ckv, ckvt, ov, batch, seq)
        kaug, vaug, kwt, vwaug = _kv_prep(z16, batch, seq, tt)
        osel = _sel_attn(qaug, z32, kaug, vaug, batch, seq, tt)
        ow = _win_attn(z16, z32, kwt, vwaug, batch, seq)

        ob = _pool(z16, pool_w[l].astype(BF16), pool_scale[l].reshape(1, D_MODEL), batch, seq, tt)
        ocn = _hgrn(z16, z32, lbs[l].reshape(HG_HEADS, 1, HG_DK), hgrn_norm[l].reshape(HG_HEADS, 1, HG_DK),
                    batch, seq, tt)

        x2 = _outproj(oc.reshape(m, D_MODEL), osel.reshape(m, D_MODEL), ow.reshape(m, D_MODEL), z16,
                      ob.reshape(m, D_MODEL), ocn.reshape(m, D_MODEL), x2,
                      w_branch[l].astype(BF16), w_out[l].astype(BF16), post_norm[l].reshape(1, D_MODEL), 256)
    return x2.reshape(batch, seq, D_MODEL)
```

```python
import functools

import jax
import jax.numpy as jnp
from jax import lax
from jax.experimental import pallas as pl
from jax.experimental.pallas import tpu as pltpu

F32 = jnp.float32
BF16 = jnp.bfloat16

D_MODEL = 1024
NORM_EPS = 1e-6
DH = 64
N_G = 4
HPG = 4
GW = HPG * DH
KVW = N_G * DH
CMP_BLOCK = 32
CMP_STRIDE = 16
CMP_LANES = 128
SEL_BLOCK = 64
N_SELECT = 16
WINDOW = 512
QT = 256
CQ = 256
FORCE_SCORE = 1.0e4
ATT_SCALE = DH ** -0.5
MASK_BIAS = -(2.0 ** 30)
POOL_WINDOWS = (2, 4, 8, 16)
POOL_GROUP = D_MODEL // len(POOL_WINDOWS)
POOL_HALO = 16
HG_HEADS = 8
HG_DK = 128
HG_CHUNK = 64
HG_SUB = 8
HG_NH = 4

Z16_Q, Z16_KS, Z16_VS, Z16_KW, Z16_VW = 0, 1024, 1280, 1536, 1792
Z16_ASILU, Z16_BV, Z16_BSILU = 2048, 3072, 4096
Z16_CQ, Z16_CI, Z16_CSILU, Z16_MERGE = 5120, 6144, 7168, 8192
Z16_W = 11264
Z32_CF, Z32_KC, Z32_VC, Z32_GATE = 0, 1024, 1280, 1536
Z32_W = 2048
GATE_PAD = 128

VMEM_LIMIT = 56 * 1024 * 1024


def _cparams(sem):
    return pltpu.CompilerParams(dimension_semantics=sem, vmem_limit_bytes=VMEM_LIMIT)


def _sigmoid(x):
    return 1.0 / (1.0 + jnp.exp(-x))


def _silu(x):
    return x * _sigmoid(x)


def _inproj_kernel(x_ref, g_ref, w_ref, o_ref, h_ref):
    @pl.when(pl.program_id(1) == 0)
    def _():
        x = x_ref[...]
        ms = jnp.mean(x * x, axis=-1, keepdims=True)
        h_ref[...] = (x * lax.rsqrt(ms + NORM_EPS) * g_ref[...]).astype(BF16)

    o_ref[...] = jnp.dot(h_ref[...], w_ref[...], preferred_element_type=F32).astype(o_ref.dtype)


def _inproj(x2, gain, w, out_dtype, tm, tn):
    m, d = x2.shape
    n = w.shape[1]
    return pl.pallas_call(
        _inproj_kernel,
        grid=(m // tm, n // tn),
        in_specs=[pl.BlockSpec((tm, d), lambda i, j: (i, 0)),
                  pl.BlockSpec((1, d), lambda i, j: (0, 0)),
                  pl.BlockSpec((d, tn), lambda i, j: (0, j))],
        out_specs=pl.BlockSpec((tm, tn), lambda i, j: (i, j)),
        out_shape=jax.ShapeDtypeStruct((m, n), out_dtype),
        scratch_shapes=[pltpu.VMEM((tm, d), BF16)],
        compiler_params=_cparams(("parallel", "arbitrary")),
        name="inproj",
    )(x2, gain, w)


def _compress_kernel(x_ref, pe_ref, w_ref, o_ref, ot_ref, *, nch):
    acc_a = jnp.zeros((nch, CMP_LANES), F32)
    acc_b = jnp.zeros((nch, CMP_LANES), F32)
    for p in range(CMP_STRIDE):
        xp = x_ref[0, pl.ds(p, nch, stride=CMP_STRIDE), :]
        a = (xp + pe_ref[0, p:p + 1, :]).astype(BF16)
        b = (xp + pe_ref[0, CMP_STRIDE + p:CMP_STRIDE + p + 1, :]).astype(BF16)
        acc_a += jnp.dot(a, w_ref[0, p], preferred_element_type=F32)
        acc_b += jnp.dot(b, w_ref[0, CMP_STRIDE + p], preferred_element_type=F32)
    c = acc_a + pltpu.roll(acc_b, nch - 1, 0)
    ct = c.T
    for g in range(CMP_LANES // DH):
        o_ref[0, 0, g] = c[:, g * DH:(g + 1) * DH].astype(BF16)
        ot_ref[0, 0, g] = ct[g * DH:(g + 1) * DH, :].astype(BF16)


def _compress(z32, pe, wbd, batch, seq):
    nch = seq // CMP_STRIDE
    z3 = z32.reshape(batch, seq, Z32_W)
    gpb = CMP_LANES // DH
    return pl.pallas_call(
        functools.partial(_compress_kernel, nch=nch),
        grid=(batch, 2, N_G // gpb),
        in_specs=[pl.BlockSpec((1, seq, CMP_LANES),
                               lambda b, s, c: (b, 0, (Z32_KC + s * KVW) // CMP_LANES + c)),
                  pl.BlockSpec((1, CMP_BLOCK, CMP_LANES), lambda b, s, c: (s, 0, 0)),
                  pl.BlockSpec((1, CMP_BLOCK, CMP_LANES, CMP_LANES), lambda b, s, c: (s, 0, 0, 0))],
        out_specs=[pl.BlockSpec((1, 1, gpb, nch, DH), lambda b, s, c: (b, s, c, 0, 0)),
                   pl.BlockSpec((1, 1, gpb, DH, nch), lambda b, s, c: (b, s, c, 0, 0))],
        out_shape=[jax.ShapeDtypeStruct((batch, 2, N_G, nch, DH), BF16),
                   jax.ShapeDtypeStruct((batch, 2, N_G, DH, nch), BF16)],
        compiler_params=_cparams(("parallel", "parallel", "parallel")),
        name="compress",
    )(z3, pe, wbd)


def _cmp_select_kernel(q_ref, ck_ref, cvt_ref, ov_ref, gate_ref, oc_ref, qa_ref, *, nch, n_s, n_sel):
    s0 = pl.program_id(2) * CQ
    qt = (q_ref[0].astype(F32).T * ATT_SCALE).astype(BF16)
    ck = ck_ref[0, 0, 0]
    cvt = cvt_ref[0, 0, 0]
    t_lane = s0 + lax.broadcasted_iota(jnp.int32, (1, CQ), 1)
    c_end = lax.broadcasted_iota(jnp.int32, (nch, 1), 0) * CMP_STRIDE + (CMP_BLOCK - 1)
    visible = c_end <= t_lane
    gates = _sigmoid(gate_ref[0])

    p_sum = jnp.zeros((nch, CQ), F32)
    outs = []
    for h in range(HPG):
        s = jnp.dot(ck, qt[h * DH:(h + 1) * DH, :], preferred_element_type=F32)
        s = jnp.where(visible, s, -jnp.inf)
        m = jnp.max(s, axis=0, keepdims=True)
        m = jnp.where(m == -jnp.inf, 0.0, m)
        e = jnp.exp(s - m)
        p = e / jnp.maximum(jnp.sum(e, axis=0, keepdims=True), 1e-30)
        p_sum += p
        oct_h = jnp.dot(cvt, p.astype(BF16), preferred_element_type=F32)
        outs.append(oct_h.T * gates[:, 3 * h:3 * h + 1])
    oc_ref[0] = jnp.concatenate(outs, axis=1).astype(BF16)

    p_hi = p_sum.astype(BF16)
    p_lo = (p_sum - p_hi.astype(F32)).astype(BF16)
    ov = ov_ref[...]
    imp = (jnp.dot(ov, p_hi, preferred_element_type=F32)
           + jnp.dot(ov, p_lo, preferred_element_type=F32))

    blk = lax.broadcasted_iota(jnp.int32, (n_s, 1), 0)
    cur = t_lane // SEL_BLOCK
    valid = blk <= cur
    force = (blk == 0) | (blk == cur) | (blk == cur - 1)
    score = jnp.where(valid, jnp.where(force, FORCE_SCORE, imp), -1.0)

    def pick(_, carry):
        sc, sel = carry
        m = jnp.max(sc, axis=0, keepdims=True)
        first = jnp.min(jnp.where(sc == m, blk, n_s), axis=0, keepdims=True)
        hit = blk == first
        return jnp.where(hit, -jnp.inf, sc), jnp.where(hit, 1.0, sel)

    _, sel = lax.fori_loop(0, n_sel, pick, (score, jnp.zeros((n_s, CQ), F32)), unroll=True)
    bias_t = jnp.where((sel > 0.5) & valid, 0.0, MASK_BIAS).astype(BF16)
    for j in range(CQ // QT):
        lanes = slice(j * QT, (j + 1) * QT)
        qa_ref[0, 0, j, 0:n_s, :] = jnp.concatenate([bias_t[:, lanes]] * HPG, axis=1)
        qa_ref[0, 0, j, n_s:n_s + DH, :] = jnp.concatenate(
            [qt[h * DH:(h + 1) * DH, lanes] for h in range(HPG)], axis=1)


def _cmp_select(z16, z32, ck, cvt, ov, batch, seq):
    nch = seq // CMP_STRIDE
    n_s = seq // SEL_BLOCK
    n_sel = min(N_SELECT, n_s)
    z3 = z16.reshape(batch, seq, Z16_W)
    g3 = z32.reshape(batch, seq, Z32_W)
    return pl.pallas_call(
        functools.partial(_cmp_select_kernel, nch=nch, n_s=n_s, n_sel=n_sel),
        grid=(batch, N_G, seq // CQ),
        in_specs=[pl.BlockSpec((1, CQ, GW), lambda b, g, i: (b, i, Z16_Q // GW + g)),
                  pl.BlockSpec((1, 1, 1, nch, DH), lambda b, g, i: (b, 0, g, 0, 0)),
                  pl.BlockSpec((1, 1, 1, DH, nch), lambda b, g, i: (b, 1, g, 0, 0)),
                  pl.BlockSpec((n_s, nch), lambda b, g, i: (0, 0)),
                  pl.BlockSpec((1, CQ, GATE_PAD), lambda b, g, i: (b, i, Z32_GATE // GATE_PAD + g))],
        out_specs=[pl.BlockSpec((1, CQ, GW), lambda b, g, i: (b, i, g)),
                   pl.BlockSpec((1, 1, CQ // QT, n_s + DH, HPG * QT), lambda b, g, i: (b, g, i, 0, 0))],
        out_shape=[jax.ShapeDtypeStruct((batch, seq, D_MODEL), BF16),
                   jax.ShapeDtypeStruct((batch, N_G, seq // QT, n_s + DH, HPG * QT), BF16)],
        compiler_params=_cparams(("parallel", "parallel", "parallel")),
        name="cmp_select",
    )(z3, ck, cvt, ov, g3)


def _kv_prep_kernel(ks_ref, vs_ref, kw_ref, vw_ref, ka_ref, va_ref, kwn_ref, vwa_ref, *, n_s, tt):
    t0 = pl.program_id(1) * tt
    ks = ks_ref[0]
    kw = kw_ref[0]
    vst = vs_ref[0].astype(F32).T
    vwt = vw_ref[0].astype(F32).T
    pos = t0 + lax.broadcasted_iota(jnp.int32, (tt, 1), 0)
    blk = lax.broadcasted_iota(jnp.int32, (1, n_s), 1)
    onehot = jnp.where(pos // SEL_BLOCK == blk, 1.0, 0.0).astype(BF16)
    ones_row = jnp.where(lax.broadcasted_iota(jnp.int32, (DH, tt), 0) == 0, 1.0, 0.0).astype(BF16)
    for g in range(N_G):
        ka_ref[0, g, :, 0:n_s] = onehot
        ka_ref[0, g, :, n_s:n_s + DH] = ks[:, g * DH:(g + 1) * DH]
        kwn_ref[0, g] = kw[:, g * DH:(g + 1) * DH]
        va_ref[0, g, 0:DH, :] = vst[g * DH:(g + 1) * DH, :].astype(BF16)
        va_ref[0, g, DH:2 * DH, :] = ones_row
        vwa_ref[0, g, 0:DH, :] = vwt[g * DH:(g + 1) * DH, :].astype(BF16)
        vwa_ref[0, g, DH:2 * DH, :] = ones_row


def _kv_prep(z16, batch, seq, tt):
    n_s = seq // SEL_BLOCK
    z3 = z16.reshape(batch, seq, Z16_W)

    def col(off):
        return pl.BlockSpec((1, tt, KVW), lambda b, i: (b, i, off // KVW))

    return pl.pallas_call(
        functools.partial(_kv_prep_kernel, n_s=n_s, tt=tt),
        grid=(batch, seq // tt),
        in_specs=[col(Z16_KS), col(Z16_VS), col(Z16_KW), col(Z16_VW)],
        out_specs=[pl.BlockSpec((1, N_G, tt, n_s + DH), lambda b, i: (b, 0, i, 0)),
                   pl.BlockSpec((1, N_G, 2 * DH, tt), lambda b, i: (b, 0, 0, i)),
                   pl.BlockSpec((1, N_G, tt, DH), lambda b, i: (b, 0, i, 0)),
                   pl.BlockSpec((1, N_G, 2 * DH, tt), lambda b, i: (b, 0, 0, i))],
        out_shape=[jax.ShapeDtypeStruct((batch, N_G, seq, n_s + DH), BF16),
                   jax.ShapeDtypeStruct((batch, N_G, 2 * DH, seq), BF16),
                   jax.ShapeDtypeStruct((batch, N_G, seq, DH), BF16),
                   jax.ShapeDtypeStruct((batch, N_G, 2 * DH, seq), BF16)],
        compiler_params=_cparams(("parallel", "parallel")),
        name="kv_prep",
    )(z3, z3, z3, z3)


def _nsa_attn_kernel(qa_ref, gate_ref, oc_ref, k_ref, v_ref, kw_ref, vw_ref, o_ref,
                     sa_ref, sb_ref, m_ref, acc_ref, accw_ref, *, tk, n_s, span):
    s0 = pl.program_id(2) * QT
    unit = 2 * tk
    n_units = s0 // unit
    u0 = pl.multiple_of(n_units * unit, unit)
    m_ref[...] = jnp.full(m_ref.shape, -jnp.inf, F32)
    acc_ref[...] = jnp.zeros(acc_ref.shape, F32)
    t_lane = s0 + lax.broadcasted_iota(jnp.int32, (1, HPG * QT), 1) % QT

    start = pl.multiple_of(jnp.maximum(s0 - WINDOW, 0), QT)
    sw = jnp.dot(kw_ref[0, 0, pl.ds(start, span), :], qa_ref[0, 0, 0, n_s:n_s + DH, :],
                 preferred_element_type=F32)
    kpos_w = start + lax.broadcasted_iota(jnp.int32, (span, 1), 0)
    sw = jnp.where((kpos_w <= t_lane) & (t_lane - kpos_w < WINDOW), sw, -jnp.inf)
    pw = jnp.exp(sw - jnp.max(sw, axis=0, keepdims=True)).astype(BF16)
    accw_ref[...] = jnp.dot(vw_ref[0, 0, :, pl.ds(start, span)], pw, preferred_element_type=F32)

    def scores(off):
        return jnp.dot(k_ref[0, 0, pl.ds(off, tk), :], qa_ref[0, 0, 0], preferred_element_type=F32)

    def causal(s, off):
        kpos = off + lax.broadcasted_iota(jnp.int32, (tk, 1), 0)
        return jnp.where(kpos <= t_lane, s, MASK_BIAS)

    def accumulate(s, off):
        m_prev = m_ref[...]
        m_new = jnp.maximum(m_prev, jnp.max(s, axis=0, keepdims=True))
        p = jnp.exp(s - m_new).astype(BF16)
        acc_ref[...] = jnp.exp(m_prev - m_new) * acc_ref[...] + jnp.dot(
            v_ref[0, 0, :, pl.ds(off, tk)], p, preferred_element_type=F32)
        m_ref[...] = m_new

    sa_ref[...] = scores(0)

    accumulate(causal(scores(u0), u0), u0)

    @pl.when(s0 >= u0 + tk)
    def _():
        off = pl.multiple_of(u0 + tk, tk)
        accumulate(causal(scores(off), off), off)

    @pl.when(n_units > 0)
    def _():
        def body(u, carry):
            off = pl.multiple_of(u * unit, unit)
            sb_ref[...] = scores(off + tk)
            accumulate(sa_ref[...], off)
            nxt = pl.multiple_of(jnp.minimum(u + 1, n_units - 1) * unit, unit)
            sa_ref[...] = scores(nxt)
            accumulate(sb_ref[...], off + tk)
            return carry

        lax.fori_loop(0, n_units, body, 0)

    gates = _sigmoid(gate_ref[0])
    outs = []
    for h in range(HPG):
        a = acc_ref[:, h * QT:(h + 1) * QT]
        aw = accw_ref[:, h * QT:(h + 1) * QT]
        o_sel = (a[0:DH, :] / a[DH:DH + 1, :]).T
        o_win = (aw[0:DH, :] / aw[DH:DH + 1, :]).T
        outs.append(o_sel * gates[:, 3 * h + 1:3 * h + 2] + o_win * gates[:, 3 * h + 2:3 * h + 3])
    o_ref[0] = (oc_ref[0].astype(F32) + jnp.concatenate(outs, axis=1)).astype(BF16)


def _nsa_attn(qaug, z32, oc, kaug, vaug, kwn, vwaug, batch, seq, tk):
    kd = kaug.shape[-1]
    span = min(WINDOW + QT, seq)
    g3 = z32.reshape(batch, seq, Z32_W)

    def per_head(shape):
        return pl.BlockSpec((1, 1) + shape, lambda b, g, i: (b, g, 0, 0))

    return pl.pallas_call(
        functools.partial(_nsa_attn_kernel, tk=tk, n_s=kd - DH, span=span),
        grid=(batch, N_G, seq // QT),
        in_specs=[pl.BlockSpec((1, 1, 1, kd, HPG * QT), lambda b, g, i: (b, g, i, 0, 0)),
                  pl.BlockSpec((1, QT, GATE_PAD), lambda b, g, i: (b, i, Z32_GATE // GATE_PAD + g)),
                  pl.BlockSpec((1, QT, GW), lambda b, g, i: (b, i, g)),
                  per_head((seq, kd)), per_head((2 * DH, seq)), per_head((seq, DH)), per_head((2 * DH, seq))],
        out_specs=pl.BlockSpec((1, QT, GW), lambda b, g, i: (b, i, g)),
        out_shape=jax.ShapeDtypeStruct((batch, seq, D_MODEL), BF16),
        scratch_shapes=[pltpu.VMEM((tk, HPG * QT), F32),
                        pltpu.VMEM((tk, HPG * QT), F32),
                        pltpu.VMEM((1, HPG * QT), F32),
                        pltpu.VMEM((2 * DH, HPG * QT), F32),
                        pltpu.VMEM((2 * DH, HPG * QT), F32)],
        compiler_params=_cparams(("parallel", "parallel", "arbitrary")),
        name="nsa_attn",
    )(qaug, g3, oc, kaug, vaug, kwn, vwaug)


def _pool_kernel(v_ref, prev_ref, gate_ref, w_ref, scale_ref, o_ref, *, tt):
    i = pl.program_id(1)
    cur = v_ref[0].astype(F32)
    prev = jnp.where(i > 0, prev_ref[0].astype(F32), 0.0)
    t = i * tt + lax.broadcasted_iota(jnp.int32, (tt, 1), 0)
    for g, w in enumerate(POOL_WINDOWS):
        sl = slice(g * POOL_GROUP, (g + 1) * POOL_GROUP)
        ext = jnp.concatenate([prev[:, sl], cur[:, sl]], axis=0)
        acc = ext
        d = 1
        while d < w:
            acc = acc + pltpu.roll(acc, d, 0)
            d *= 2
        cnt = jnp.minimum(t + 1, w).astype(F32)
        pooled = acc[POOL_HALO:, :] / cnt - cur[:, sl]
        mixed = jnp.dot(pooled.astype(BF16), w_ref[g], preferred_element_type=F32)
        o_ref[0, :, sl] = (mixed * scale_ref[:, sl] * _silu(gate_ref[0, :, sl].astype(F32))).astype(BF16)


def _pool(z16, w_pool, scale, batch, seq, tt):
    z3 = z16.reshape(batch, seq, Z16_W)
    hb = tt // POOL_HALO
    return pl.pallas_call(
        functools.partial(_pool_kernel, tt=tt),
        grid=(batch, seq // tt),
        in_specs=[pl.BlockSpec((1, tt, D_MODEL), lambda b, i: (b, i, Z16_BV // D_MODEL)),
                  pl.BlockSpec((1, POOL_HALO, D_MODEL),
                               lambda b, i: (b, jnp.maximum(i * hb - 1, 0), Z16_BV // D_MODEL)),
                  pl.BlockSpec((1, tt, D_MODEL), lambda b, i: (b, i, Z16_BSILU // D_MODEL)),
                  pl.BlockSpec((len(POOL_WINDOWS), POOL_GROUP, POOL_GROUP), lambda b, i: (0, 0, 0)),
                  pl.BlockSpec((1, D_MODEL), lambda b, i: (0, 0))],
        out_specs=pl.BlockSpec((1, tt, D_MODEL), lambda b, i: (b, i, 0)),
        out_shape=jax.ShapeDtypeStruct((batch, seq, D_MODEL), BF16),
        compiler_params=_cparams(("parallel", "parallel")),
        name="pool",
    )(z3, z3, z3, w_pool, scale)


def _hgrn_kernel(q_ref, f_ref, i_ref, gate_ref, lb_ref, gain_ref, o_ref, st_ref, *, n_chunks, nh):
    c, sub = HG_CHUNK, HG_SUB

    @pl.when(pl.program_id(2) == 0)
    def _():
        st_ref[...] = jnp.zeros(st_ref.shape, F32)

    row = lax.broadcasted_iota(jnp.int32, (c, 1), 0)
    tri = jnp.where(lax.broadcasted_iota(jnp.int32, (c, c), 1) <= row, 1.0, 0.0).astype(BF16)
    col_c = lax.broadcasted_iota(jnp.int32, (1, c), 1)
    sub_row = lax.broadcasted_iota(jnp.int32, (sub, 1), 0)
    nt = (((1,), (1,)), ((), ()))

    def head_chunk(r0, hd):
        cols = slice(hd * HG_DK, (hd + 1) * HG_DK)
        lb = lb_ref[0, :, cols]
        f = lb + (1.0 - lb) * _sigmoid(f_ref[0, pl.ds(r0, c), cols])
        lf = jnp.log(f)
        k = 1.0 - f
        q = q_ref[0, pl.ds(r0, c), cols].astype(F32)
        v = i_ref[0, pl.ds(r0, c), cols]
        hi = lf.astype(BF16)
        r1 = lf - hi.astype(F32)
        mid = r1.astype(BF16)
        lo = (r1 - mid.astype(F32)).astype(BF16)
        a = (jnp.dot(tri, hi, preferred_element_type=F32)
             + jnp.dot(tri, mid, preferred_element_type=F32)
             + jnp.dot(tri, lo, preferred_element_type=F32))

        st = st_ref[hd]
        o = lax.dot_general((q * jnp.exp(a)).astype(BF16), st.astype(BF16), nt, preferred_element_type=F32)

        att = jnp.zeros((c, c), F32)
        half = c // 2
        while half >= sub:
            pair = 2 * half
            ref = jnp.concatenate(
                [jnp.broadcast_to(a[p * pair + half - 1:p * pair + half, :], (pair, HG_DK))
                 for p in range(c // pair)], axis=0)
            right = (row % pair) >= half
            qt = jnp.where(right, q * jnp.exp(jnp.minimum(a - ref, 0.0)), 0.0)
            kt = jnp.where(right, 0.0, k * jnp.exp(jnp.minimum(ref - a, 0.0)))
            lvl = lax.dot_general(qt.astype(BF16), kt.astype(BF16), nt, preferred_element_type=F32)
            att = att + jnp.where((row // pair) == (col_c // pair), lvl, 0.0)
            half //= 2
        diag = []
        for b in range(c // sub):
            ab = a[b * sub:(b + 1) * sub, :]
            qb = q[b * sub:(b + 1) * sub, :]
            kb = k[b * sub:(b + 1) * sub, :]
            blk = jnp.zeros((sub, c), F32)
            for s in range(sub):
                e = jnp.exp(jnp.minimum(ab - ab[s:s + 1, :], 0.0))
                colv = jnp.sum(qb * kb[s:s + 1, :] * e, axis=1, keepdims=True)
                colv = jnp.where(sub_row >= s, colv, 0.0)
                blk = blk + jnp.where(col_c == b * sub + s, colv, 0.0)
            diag.append(blk)
        att = att + jnp.concatenate(diag, axis=0)
        o = o + jnp.dot(att.astype(BF16), v, preferred_element_type=F32)

        a_last = a[c - 1:c, :]
        kd = (k * jnp.exp(a_last - a)).astype(BF16)
        st_ref[hd] = st * jnp.exp(a_last) + lax.dot_general(
            v, kd, (((0,), (0,)), ((), ())), preferred_element_type=F32)

        ms = jnp.mean(o * o, axis=-1, keepdims=True)
        y = o * lax.rsqrt(ms + NORM_EPS) * gain_ref[0, :, cols]
        o_ref[0, pl.ds(r0, c), cols] = (y * _silu(gate_ref[0, pl.ds(r0, c), cols].astype(F32))).astype(BF16)

    def chunk(n, carry):
        r0 = pl.multiple_of(n * c, c)
        for hd in range(nh):
            head_chunk(r0, hd)
        return carry

    lax.fori_loop(0, n_chunks, chunk, 0)


def _hgrn(z16, z32, lb, gain, batch, seq, tc, nh):
    z3 = z16.reshape(batch, seq, Z16_W)
    f3 = z32.reshape(batch, seq, Z32_W)
    w = nh * HG_DK

    def col16(off):
        return pl.BlockSpec((1, tc, w), lambda b, h, i: (b, i, off // w + h))

    return pl.pallas_call(
        functools.partial(_hgrn_kernel, n_chunks=tc // HG_CHUNK, nh=nh),
        grid=(batch, HG_HEADS // nh, seq // tc),
        in_specs=[col16(Z16_CQ),
                  pl.BlockSpec((1, tc, w), lambda b, h, i: (b, i, Z32_CF // w + h)),
                  col16(Z16_CI), col16(Z16_CSILU),
                  pl.BlockSpec((1, 1, w), lambda b, h, i: (h, 0, 0)),
                  pl.BlockSpec((1, 1, w), lambda b, h, i: (h, 0, 0))],
        out_specs=pl.BlockSpec((1, tc, w), lambda b, h, i: (b, i, h)),
        out_shape=jax.ShapeDtypeStruct((batch, seq, D_MODEL), BF16),
        scratch_shapes=[pltpu.VMEM((nh, HG_DK, HG_DK), F32)],
        compiler_params=_cparams(("parallel", "parallel", "arbitrary")),
        name="hgrn",
    )(z3, f3, z3, z3, lb, gain)


def _outproj_kernel(oa_ref, asilu_ref, ob_ref, ocn_ref, m0_ref, m1_ref, m2_ref,
                    x_ref, wb_ref, wo_ref, post_ref, o_ref):
    oa = (oa_ref[...].astype(F32) * _silu(asilu_ref[...].astype(F32))).astype(BF16)
    merged = _sigmoid(m0_ref[...].astype(F32)) * jnp.dot(oa, wb_ref[0], preferred_element_type=F32)
    merged += _sigmoid(m1_ref[...].astype(F32)) * jnp.dot(ob_ref[...], wb_ref[1], preferred_element_type=F32)
    merged += _sigmoid(m2_ref[...].astype(F32)) * jnp.dot(ocn_ref[...], wb_ref[2], preferred_element_type=F32)
    out = jnp.dot(merged.astype(BF16), wo_ref[...], preferred_element_type=F32)
    ms = jnp.mean(out * out, axis=-1, keepdims=True)
    o_ref[...] = x_ref[...] + out * lax.rsqrt(ms + NORM_EPS) * post_ref[...]


def _outproj(oa, z16, ob, ocn, x2, wb, wo, post, tm):
    m = x2.shape[0]

    def rows(j=0):
        return pl.BlockSpec((tm, D_MODEL), lambda i: (i, j))

    return pl.pallas_call(
        _outproj_kernel,
        grid=(m // tm,),
        in_specs=[rows(), rows(Z16_ASILU // D_MODEL), rows(), rows(),
                  rows(Z16_MERGE // D_MODEL), rows(Z16_MERGE // D_MODEL + 1), rows(Z16_MERGE // D_MODEL + 2),
                  rows(),
                  pl.BlockSpec((3, D_MODEL, D_MODEL), lambda i: (0, 0, 0)),
                  pl.BlockSpec((D_MODEL, D_MODEL), lambda i: (0, 0)),
                  pl.BlockSpec((1, D_MODEL), lambda i: (0, 0))],
        out_specs=rows(),
        out_shape=jax.ShapeDtypeStruct((m, D_MODEL), F32),
        compiler_params=_cparams(("parallel",)),
        name="outproj",
    )(oa, z16, ob, ocn, z16, z16, z16, x2, wb, wo, post)


def _split_w_in(w):
    sizes = (1024, 256, 256, 256, 256, 256, 256, 3 * N_G * HPG, 1024, 1024, 1024, 1024, 1024, 1024, 1024, 3072)
    names = ("a_q", "a_kc", "a_vc", "a_ks", "a_vs", "a_kw", "a_vw", "a_gate", "a_silu", "b_v", "b_silu",
             "c_q", "c_f", "c_i", "c_silu", "merge")
    parts, off = {}, 0
    for n, s in zip(names, sizes):
        parts[n] = w[:, off:off + s]
        off += s
    w16 = jnp.concatenate([parts[n] for n in ("a_q", "a_ks", "a_vs", "a_kw", "a_vw", "a_silu", "b_v", "b_silu",
                                              "c_q", "c_i", "c_silu", "merge")], axis=1)
    gate = parts["a_gate"].reshape(D_MODEL, N_G, 3 * HPG)
    gate = jnp.pad(gate, ((0, 0), (0, 0), (0, GATE_PAD - 3 * HPG))).reshape(D_MODEL, N_G * GATE_PAD)
    w32 = jnp.concatenate([parts["c_f"], parts["a_kc"], parts["a_vc"], gate], axis=1)
    return w16.astype(BF16), w32.astype(BF16)


def _block_diag_heads(w):
    n = CMP_LANES // DH
    eye = jnp.eye(n, dtype=w.dtype)
    return jnp.einsum("gh,pde->pgdhe", eye, w).reshape(CMP_BLOCK, CMP_LANES, CMP_LANES)


def _overlap_matrix(seq):
    n_c = seq // CMP_STRIDE
    n_s = seq // SEL_BLOCK
    c_start = jnp.arange(n_c) * CMP_STRIDE
    s_start = jnp.arange(n_s) * SEL_BLOCK
    ov = jnp.clip(jnp.minimum(c_start[None, :] + CMP_BLOCK, s_start[:, None] + SEL_BLOCK)
                  - jnp.maximum(c_start[None, :], s_start[:, None]), 0).astype(F32) / CMP_BLOCK
    return ov.astype(BF16)


def kernel(x, pre_norm, w_in, cmp_pe_k, cmp_pe_v, cmp_w_k, cmp_w_v, pool_w, pool_scale, hgrn_lb_logits, hgrn_norm,
           w_branch, w_out, post_norm):
    batch, seq, _ = x.shape
    depth = w_in.shape[0]
    m = batch * seq
    probs = jax.nn.softmax(hgrn_lb_logits.astype(F32), axis=0)
    lbs = jnp.cumsum(probs, axis=0) - probs[0]
    ov = _overlap_matrix(seq)
    tm_in = min(2048, m)
    tt = min(512, seq)

    x2 = x.reshape(m, D_MODEL)
    for l in range(depth):
        w16, w32 = _split_w_in(w_in[l])
        gain = pre_norm[l].reshape(1, D_MODEL)
        z16 = _inproj(x2, gain, w16, BF16, tm_in, 1024)
        z32 = _inproj(x2, gain, w32, F32, tm_in, 1024)

        pe = jnp.stack([jnp.tile(cmp_pe_k[l], (1, CMP_LANES // DH)), jnp.tile(cmp_pe_v[l], (1, CMP_LANES // DH))])
        wbd = jnp.stack([_block_diag_heads(cmp_w_k[l]), _block_diag_heads(cmp_w_v[l])]).astype(BF16)
        ckv, ckvt = _compress(z32, pe, wbd, batch, seq)
        oc, qaug = _cmp_select(z16, z32, ckv, ckvt, ov, batch, seq)
        kaug, vaug, kwn, vwaug = _kv_prep(z16, batch, seq, tt)
        oa = _nsa_attn(qaug, z32, oc, kaug, vaug, kwn, vwaug, batch, seq, tt)

        ob = _pool(z16, pool_w[l].astype(BF16), pool_scale[l].reshape(1, D_MODEL), batch, seq, tt)
        ocn = _hgrn(z16, z32, lbs[l].reshape(HG_HEADS // HG_NH, 1, HG_NH * HG_DK),
                    hgrn_norm[l].reshape(HG_HEADS // HG_NH, 1, HG_NH * HG_DK), batch, seq, tt, HG_NH)

        x2 = _outproj(oa.reshape(m, D_MODEL), z16, ob.reshape(m, D_MODEL), ocn.reshape(m, D_MODEL), x2,
                      w_branch[l].astype(BF16), w_out[l].astype(BF16), post_norm[l].reshape(1, D_MODEL), 256)
    return x2.reshape(batch, seq, D_MODEL)
```

```python
import functools

import jax
import jax.numpy as jnp
from jax import lax
from jax.experimental import pallas as pl
from jax.experimental.pallas import tpu as pltpu

F32 = jnp.float32
BF16 = jnp.bfloat16

D_MODEL = 1024
NORM_EPS = 1e-6
DH = 64
N_G = 4
HPG = 4
GW = HPG * DH
KVW = N_G * DH
CMP_BLOCK = 32
CMP_STRIDE = 16
CMP_LANES = 128
SEL_BLOCK = 64
N_SELECT = 16
WINDOW = 512
QT = 256
SEL_TK = 512
CQ = 256
CMP_ROW_LEVELS = 4
LANES = 128
FORCE_SCORE = 1.0e4
ATT_SCALE = DH ** -0.5
MASK_BIAS = -(2.0 ** 30)
POOL_WINDOWS = (2, 4, 8, 16)
POOL_GROUP = D_MODEL // len(POOL_WINDOWS)
POOL_HALO = 16
HG_HEADS = 8
HG_DK = 128
HG_CHUNK = 64
HG_SUB = 8
HG_NH = 4

Z16_Q, Z16_KS, Z16_VS, Z16_KW, Z16_VW = 0, 1024, 1280, 1536, 1792
Z16_ASILU, Z16_BV, Z16_BSILU = 2048, 3072, 4096
Z16_CQ, Z16_CI, Z16_CSILU, Z16_MERGE = 5120, 6144, 7168, 8192
Z16_W = 11264
Z32_CF, Z32_KC, Z32_VC, Z32_GATE = 0, 1024, 1280, 1536
Z32_W = 2048
GATE_PAD = 128

VMEM_LIMIT = 56 * 1024 * 1024


def _cparams(sem):
    return pltpu.CompilerParams(dimension_semantics=sem, vmem_limit_bytes=VMEM_LIMIT)


def _sigmoid(x):
    return 1.0 / (1.0 + jnp.exp(-x))


def _silu(x):
    return x * _sigmoid(x)


def _inproj_kernel(x_ref, g_ref, w_ref, o_ref, h_ref):
    @pl.when(pl.program_id(1) == 0)
    def _():
        x = x_ref[...]
        ms = jnp.mean(x * x, axis=-1, keepdims=True)
        h_ref[...] = (x * lax.rsqrt(ms + NORM_EPS) * g_ref[...]).astype(BF16)

    o_ref[...] = jnp.dot(h_ref[...], w_ref[...], preferred_element_type=F32).astype(o_ref.dtype)


def _inproj(x2, gain, w, out_dtype, tm, tn):
    m, d = x2.shape
    n = w.shape[1]
    return pl.pallas_call(
        _inproj_kernel,
        grid=(m // tm, n // tn),
        in_specs=[pl.BlockSpec((tm, d), lambda i, j: (i, 0)),
                  pl.BlockSpec((1, d), lambda i, j: (0, 0)),
                  pl.BlockSpec((d, tn), lambda i, j: (0, j))],
        out_specs=pl.BlockSpec((tm, tn), lambda i, j: (i, j)),
        out_shape=jax.ShapeDtypeStruct((m, n), out_dtype),
        scratch_shapes=[pltpu.VMEM((tm, d), BF16)],
        compiler_params=_cparams(("parallel", "arbitrary")),
        name="inproj",
    )(x2, gain, w)


def _compress_kernel(x_ref, pe_ref, w_ref, o_ref, ot_ref, *, nch):
    acc_a = jnp.zeros((nch, CMP_LANES), F32)
    acc_b = jnp.zeros((nch, CMP_LANES), F32)
    for p in range(CMP_STRIDE):
        xp = x_ref[0, pl.ds(p, nch, stride=CMP_STRIDE), :]
        a = (xp + pe_ref[0, p:p + 1, :]).astype(BF16)
        b = (xp + pe_ref[0, CMP_STRIDE + p:CMP_STRIDE + p + 1, :]).astype(BF16)
        acc_a += jnp.dot(a, w_ref[0, p], preferred_element_type=F32)
        acc_b += jnp.dot(b, w_ref[0, CMP_STRIDE + p], preferred_element_type=F32)
    c = acc_a + pltpu.roll(acc_b, nch - 1, 0)
    ct = c.T
    for g in range(CMP_LANES // DH):
        o_ref[0, 0, g] = c[:, g * DH:(g + 1) * DH].astype(BF16)
        ot_ref[0, 0, g] = ct[g * DH:(g + 1) * DH, :].astype(BF16)


def _compress(z32, pe, wbd, batch, seq):
    nch = seq // CMP_STRIDE
    z3 = z32.reshape(batch, seq, Z32_W)
    gpb = CMP_LANES // DH
    return pl.pallas_call(
        functools.partial(_compress_kernel, nch=nch),
        grid=(batch, 2, N_G // gpb),
        in_specs=[pl.BlockSpec((1, seq, CMP_LANES),
                               lambda b, s, c: (b, 0, (Z32_KC + s * KVW) // CMP_LANES + c)),
                  pl.BlockSpec((1, CMP_BLOCK, CMP_LANES), lambda b, s, c: (s, 0, 0)),
                  pl.BlockSpec((1, CMP_BLOCK, CMP_LANES, CMP_LANES), lambda b, s, c: (s, 0, 0, 0))],
        out_specs=[pl.BlockSpec((1, 1, gpb, nch, DH), lambda b, s, c: (b, s, c, 0, 0)),
                   pl.BlockSpec((1, 1, gpb, DH, nch), lambda b, s, c: (b, s, c, 0, 0))],
        out_shape=[jax.ShapeDtypeStruct((batch, 2, N_G, nch, DH), BF16),
                   jax.ShapeDtypeStruct((batch, 2, N_G, DH, nch), BF16)],
        compiler_params=_cparams(("parallel", "parallel", "parallel")),
        name="compress",
    )(z3, pe, wbd)


def _cmp_select_kernel(q_ref, ck_ref, cvt_ref, ov_ref, gate_ref, oc_ref, qa_ref, imp_ref, *, nch, n_s, n_sel):
    s0 = pl.program_id(2) * CQ
    qt = (q_ref[0].astype(F32).T * ATT_SCALE).astype(BF16)
    t_lane = s0 + lax.broadcasted_iota(jnp.int32, (1, CQ), 1)
    n_vis = (s0 + CQ - CMP_BLOCK) // CMP_STRIDE + 1

    def attend(rows):
        ck = ck_ref[0, 0, 0, 0:rows, :]
        cvt = cvt_ref[0, 0, 0, :, 0:rows]
        c_end = lax.broadcasted_iota(jnp.int32, (rows, 1), 0) * CMP_STRIDE + (CMP_BLOCK - 1)
        visible = c_end <= t_lane
        gates = _sigmoid(gate_ref[0])
        p_sum = jnp.zeros((rows, CQ), F32)
        outs = []
        for h in range(HPG):
            s = jnp.dot(ck, qt[h * DH:(h + 1) * DH, :], preferred_element_type=F32)
            s = jnp.where(visible, s, -jnp.inf)
            m = jnp.max(s, axis=0, keepdims=True)
            m = jnp.where(m == -jnp.inf, 0.0, m)
            e = jnp.exp(s - m)
            p = e / jnp.maximum(jnp.sum(e, axis=0, keepdims=True), 1e-30)
            p_sum += p
            oct_h = jnp.dot(cvt, p.astype(BF16), preferred_element_type=F32)
            outs.append(oct_h.T * gates[:, 3 * h:3 * h + 1])
        oc_ref[0] = jnp.concatenate(outs, axis=1).astype(BF16)
        p_hi = p_sum.astype(BF16)
        p_lo = (p_sum - p_hi.astype(F32)).astype(BF16)
        ov = ov_ref[:, 0:rows]
        imp_ref[...] = (jnp.dot(ov, p_hi, preferred_element_type=F32)
                        + jnp.dot(ov, p_lo, preferred_element_type=F32))

    quantum = nch // CMP_ROW_LEVELS
    for j in range(CMP_ROW_LEVELS):
        rows = quantum * (j + 1)
        if j == 0:
            cond = n_vis <= rows
        else:
            cond = (n_vis > rows - quantum) & (n_vis <= rows)
        pl.when(cond)(functools.partial(attend, rows))

    blk = lax.broadcasted_iota(jnp.int32, (n_s, 1), 0)

    def pick(_, sc):
        m = jnp.max(sc, axis=0, keepdims=True)
        first = jnp.min(jnp.where(sc == m, blk, n_s), axis=0, keepdims=True)
        return jnp.where(blk == first, -jnp.inf, sc)

    biases = []
    for j in range(CQ // LANES):
        lanes = slice(j * LANES, (j + 1) * LANES)
        cur = t_lane[:, lanes] // SEL_BLOCK
        valid = blk <= cur
        force = (blk == 0) | (blk == cur) | (blk == cur - 1)
        score = jnp.where(valid, jnp.where(force, FORCE_SCORE, imp_ref[:, lanes]), -1.0)
        sc = lax.fori_loop(0, n_sel, pick, score, unroll=True)
        biases.append(jnp.where((sc == -jnp.inf) & valid, 0.0, MASK_BIAS).astype(BF16))
    bias_t = jnp.concatenate(biases, axis=1)
    for j in range(CQ // QT):
        lanes = slice(j * QT, (j + 1) * QT)
        qa_ref[0, 0, j, 0:n_s, :] = jnp.concatenate([bias_t[:, lanes]] * HPG, axis=1)
        qa_ref[0, 0, j, n_s:n_s + DH, :] = jnp.concatenate(
            [qt[h * DH:(h + 1) * DH, lanes] for h in range(HPG)], axis=1)


def _cmp_select(z16, z32, ck, cvt, ov, batch, seq):
    nch = seq // CMP_STRIDE
    n_s = seq // SEL_BLOCK
    n_sel = min(N_SELECT, n_s)
    z3 = z16.reshape(batch, seq, Z16_W)
    g3 = z32.reshape(batch, seq, Z32_W)
    return pl.pallas_call(
        functools.partial(_cmp_select_kernel, nch=nch, n_s=n_s, n_sel=n_sel),
        grid=(batch, N_G, seq // CQ),
        in_specs=[pl.BlockSpec((1, CQ, GW), lambda b, g, i: (b, i, Z16_Q // GW + g)),
                  pl.BlockSpec((1, 1, 1, nch, DH), lambda b, g, i: (b, 0, g, 0, 0)),
                  pl.BlockSpec((1, 1, 1, DH, nch), lambda b, g, i: (b, 1, g, 0, 0)),
                  pl.BlockSpec((n_s, nch), lambda b, g, i: (0, 0)),
                  pl.BlockSpec((1, CQ, GATE_PAD), lambda b, g, i: (b, i, Z32_GATE // GATE_PAD + g))],
        out_specs=[pl.BlockSpec((1, CQ, GW), lambda b, g, i: (b, i, g)),
                   pl.BlockSpec((1, 1, CQ // QT, n_s + DH, HPG * QT), lambda b, g, i: (b, g, i, 0, 0))],
        out_shape=[jax.ShapeDtypeStruct((batch, seq, D_MODEL), BF16),
                   jax.ShapeDtypeStruct((batch, N_G, seq // QT, n_s + DH, HPG * QT), BF16)],
        scratch_shapes=[pltpu.VMEM((n_s, CQ), F32)],
        compiler_params=_cparams(("parallel", "parallel", "parallel")),
        name="cmp_select",
    )(z3, ck, cvt, ov, g3)


def _kv_prep_kernel(ks_ref, vs_ref, kw_ref, vw_ref, ka_ref, va_ref, kwn_ref, vwa_ref, *, n_s, tt):
    t0 = pl.program_id(1) * tt
    ks = ks_ref[0]
    kw = kw_ref[0]
    vst = vs_ref[0].astype(F32).T
    vwt = vw_ref[0].astype(F32).T
    pos = t0 + lax.broadcasted_iota(jnp.int32, (tt, 1), 0)
    blk = lax.broadcasted_iota(jnp.int32, (1, n_s), 1)
    onehot = jnp.where(pos // SEL_BLOCK == blk, 1.0, 0.0).astype(BF16)
    ones_row = jnp.where(lax.broadcasted_iota(jnp.int32, (DH, tt), 0) == 0, 1.0, 0.0).astype(BF16)
    for g in range(N_G):
        ka_ref[0, g, :, 0:n_s] = onehot
        ka_ref[0, g, :, n_s:n_s + DH] = ks[:, g * DH:(g + 1) * DH]
        kwn_ref[0, g] = kw[:, g * DH:(g + 1) * DH]
        va_ref[0, g, 0:DH, :] = vst[g * DH:(g + 1) * DH, :].astype(BF16)
        va_ref[0, g, DH:2 * DH, :] = ones_row
        vwa_ref[0, g, 0:DH, :] = vwt[g * DH:(g + 1) * DH, :].astype(BF16)
        vwa_ref[0, g, DH:2 * DH, :] = ones_row


def _kv_prep(z16, batch, seq, tt):
    n_s = seq // SEL_BLOCK
    z3 = z16.reshape(batch, seq, Z16_W)

    def col(off):
        return pl.BlockSpec((1, tt, KVW), lambda b, i: (b, i, off // KVW))

    return pl.pallas_call(
        functools.partial(_kv_prep_kernel, n_s=n_s, tt=tt),
        grid=(batch, seq // tt),
        in_specs=[col(Z16_KS), col(Z16_VS), col(Z16_KW), col(Z16_VW)],
        out_specs=[pl.BlockSpec((1, N_G, tt, n_s + DH), lambda b, i: (b, 0, i, 0)),
                   pl.BlockSpec((1, N_G, 2 * DH, tt), lambda b, i: (b, 0, 0, i)),
                   pl.BlockSpec((1, N_G, tt, DH), lambda b, i: (b, 0, i, 0)),
                   pl.BlockSpec((1, N_G, 2 * DH, tt), lambda b, i: (b, 0, 0, i))],
        out_shape=[jax.ShapeDtypeStruct((batch, N_G, seq, n_s + DH), BF16),
                   jax.ShapeDtypeStruct((batch, N_G, 2 * DH, seq), BF16),
                   jax.ShapeDtypeStruct((batch, N_G, seq, DH), BF16),
                   jax.ShapeDtypeStruct((batch, N_G, 2 * DH, seq), BF16)],
        compiler_params=_cparams(("parallel", "parallel")),
        name="kv_prep",
    )(z3, z3, z3, z3)


def _nsa_attn_kernel(qa_ref, gate_ref, oc_ref, k_ref, v_ref, kw_ref, vw_ref, o_ref,
                     sa_ref, sb_ref, m_ref, acc_ref, accw_ref, *, tk, n_s, span):
    s0 = pl.program_id(2) * QT
    unit = 2 * tk
    n_units = s0 // unit
    u0 = pl.multiple_of(n_units * unit, unit)
    m_ref[...] = jnp.full(m_ref.shape, -jnp.inf, F32)
    acc_ref[...] = jnp.zeros(acc_ref.shape, F32)
    t_lane = s0 + lax.broadcasted_iota(jnp.int32, (1, HPG * QT), 1) % QT

    start = pl.multiple_of(jnp.maximum(s0 - WINDOW, 0), QT)
    sw = jnp.dot(kw_ref[0, 0, pl.ds(start, span), :], qa_ref[0, 0, 0, n_s:n_s + DH, :],
                 preferred_element_type=F32)
    kpos_w = start + lax.broadcasted_iota(jnp.int32, (span, 1), 0)
    sw = jnp.where((kpos_w <= t_lane) & (t_lane - kpos_w < WINDOW), sw, -jnp.inf)
    pw = jnp.exp(sw - jnp.max(sw, axis=0, keepdims=True)).astype(BF16)
    accw_ref[...] = jnp.dot(vw_ref[0, 0, :, pl.ds(start, span)], pw, preferred_element_type=F32)

    def scores(off):
        return jnp.dot(k_ref[0, 0, pl.ds(off, tk), :], qa_ref[0, 0, 0], preferred_element_type=F32)

    def causal(s, off):
        kpos = off + lax.broadcasted_iota(jnp.int32, (tk, 1), 0)
        return jnp.where(kpos <= t_lane, s, MASK_BIAS)

    def accumulate(s, off):
        m_prev = m_ref[...]
        m_new = jnp.maximum(m_prev, jnp.max(s, axis=0, keepdims=True))
        p = jnp.exp(s - m_new).astype(BF16)
        acc_ref[...] = jnp.exp(m_prev - m_new) * acc_ref[...] + jnp.dot(
            v_ref[0, 0, :, pl.ds(off, tk)], p, preferred_element_type=F32)
        m_ref[...] = m_new

    sa_ref[...] = scores(0)

    accumulate(causal(scores(u0), u0), u0)

    @pl.when(s0 >= u0 + tk)
    def _():
        off = pl.multiple_of(u0 + tk, tk)
        accumulate(causal(scores(off), off), off)

    @pl.when(n_units > 0)
    def _():
        def body(u, carry):
            off = pl.multiple_of(u * unit, unit)
            sb_ref[...] = scores(off + tk)
            accumulate(sa_ref[...], off)
            nxt = pl.multiple_of(jnp.minimum(u + 1, n_units - 1) * unit, unit)
            sa_ref[...] = scores(nxt)
            accumulate(sb_ref[...], off + tk)
            return carry

        lax.fori_loop(0, n_units, body, 0)

    gates_t = _sigmoid(gate_ref[0]).T
    outs = []
    for h in range(HPG):
        a = acc_ref[:, h * QT:(h + 1) * QT]
        aw = accw_ref[:, h * QT:(h + 1) * QT]
        w_sel = gates_t[3 * h + 1:3 * h + 2, :] / a[DH:DH + 1, :]
        w_win = gates_t[3 * h + 2:3 * h + 3, :] / aw[DH:DH + 1, :]
        outs.append((a[0:DH, :] * w_sel + aw[0:DH, :] * w_win).T)
    o_ref[0] = (oc_ref[0].astype(F32) + jnp.concatenate(outs, axis=1)).astype(BF16)


def _nsa_attn(qaug, z32, oc, kaug, vaug, kwn, vwaug, batch, seq, tk):
    kd = kaug.shape[-1]
    span = min(WINDOW + QT, seq)
    g3 = z32.reshape(batch, seq, Z32_W)

    def per_head(shape):
        return pl.BlockSpec((1, 1) + shape, lambda b, g, i: (b, g, 0, 0))

    return pl.pallas_call(
        functools.partial(_nsa_attn_kernel, tk=tk, n_s=kd - DH, span=span),
        grid=(batch, N_G, seq // QT),
        in_specs=[pl.BlockSpec((1, 1, 1, kd, HPG * QT), lambda b, g, i: (b, g, i, 0, 0)),
                  pl.BlockSpec((1, QT, GATE_PAD), lambda b, g, i: (b, i, Z32_GATE // GATE_PAD + g)),
                  pl.BlockSpec((1, QT, GW), lambda b, g, i: (b, i, g)),
                  per_head((seq, kd)), per_head((2 * DH, seq)), per_head((seq, DH)), per_head((2 * DH, seq))],
        out_specs=pl.BlockSpec((1, QT, GW), lambda b, g, i: (b, i, g)),
        out_shape=jax.ShapeDtypeStruct((batch, seq, D_MODEL), BF16),
        scratch_shapes=[pltpu.VMEM((tk, HPG * QT), F32),
                        pltpu.VMEM((tk, HPG * QT), F32),
                        pltpu.VMEM((1, HPG * QT), F32),
                        pltpu.VMEM((2 * DH, HPG * QT), F32),
                        pltpu.VMEM((2 * DH, HPG * QT), F32)],
        compiler_params=_cparams(("parallel", "parallel", "arbitrary")),
        name="nsa_attn",
    )(qaug, g3, oc, kaug, vaug, kwn, vwaug)


def _pool_kernel(v_ref, prev_ref, gate_ref, w_ref, scale_ref, o_ref, *, tt):
    i = pl.program_id(1)
    cur = v_ref[0].astype(F32)
    prev = jnp.where(i > 0, prev_ref[0].astype(F32), 0.0)
    t = i * tt + lax.broadcasted_iota(jnp.int32, (tt, 1), 0)
    for g, w in enumerate(POOL_WINDOWS):
        sl = slice(g * POOL_GROUP, (g + 1) * POOL_GROUP)
        ext = jnp.concatenate([prev[:, sl], cur[:, sl]], axis=0)
        acc = ext
        d = 1
        while d < w:
            acc = acc + pltpu.roll(acc, d, 0)
            d *= 2
        cnt = jnp.minimum(t + 1, w).astype(F32)
        pooled = acc[POOL_HALO:, :] / cnt - cur[:, sl]
        mixed = jnp.dot(pooled.astype(BF16), w_ref[g], preferred_element_type=F32)
        o_ref[0, :, sl] = (mixed * scale_ref[:, sl] * _silu(gate_ref[0, :, sl].astype(F32))).astype(BF16)


def _pool(z16, w_pool, scale, batch, seq, tt):
    z3 = z16.reshape(batch, seq, Z16_W)
    hb = tt // POOL_HALO
    return pl.pallas_call(
        functools.partial(_pool_kernel, tt=tt),
        grid=(batch, seq // tt),
        in_specs=[pl.BlockSpec((1, tt, D_MODEL), lambda b, i: (b, i, Z16_BV // D_MODEL)),
                  pl.BlockSpec((1, POOL_HALO, D_MODEL),
                               lambda b, i: (b, jnp.maximum(i * hb - 1, 0), Z16_BV // D_MODEL)),
                  pl.BlockSpec((1, tt, D_MODEL), lambda b, i: (b, i, Z16_BSILU // D_MODEL)),
                  pl.BlockSpec((len(POOL_WINDOWS), POOL_GROUP, POOL_GROUP), lambda b, i: (0, 0, 0)),
                  pl.BlockSpec((1, D_MODEL), lambda b, i: (0, 0))],
        out_specs=pl.BlockSpec((1, tt, D_MODEL), lambda b, i: (b, i, 0)),
        out_shape=jax.ShapeDtypeStruct((batch, seq, D_MODEL), BF16),
        compiler_params=_cparams(("parallel", "parallel")),
        name="pool",
    )(z3, z3, z3, w_pool, scale)


def _hgrn_kernel(q_ref, f_ref, i_ref, gate_ref, lb_ref, gain_ref, o_ref, st_ref, *, n_chunks, nh):
    c, sub = HG_CHUNK, HG_SUB

    @pl.when(pl.program_id(2) == 0)
    def _():
        st_ref[...] = jnp.zeros(st_ref.shape, F32)

    row = lax.broadcasted_iota(jnp.int32, (c, 1), 0)
    tri = jnp.where(lax.broadcasted_iota(jnp.int32, (c, c), 1) <= row, 1.0, 0.0).astype(BF16)
    col_c = lax.broadcasted_iota(jnp.int32, (1, c), 1)
    nt = (((1,), (1,)), ((), ()))

    def head_chunk(r0, hd):
        cols = slice(hd * HG_DK, (hd + 1) * HG_DK)
        lb = lb_ref[0, :, cols]
        f = lb + (1.0 - lb) * _sigmoid(f_ref[0, pl.ds(r0, c), cols])
        lf = jnp.log(f)
        k = 1.0 - f
        q = q_ref[0, pl.ds(r0, c), cols].astype(F32)
        v = i_ref[0, pl.ds(r0, c), cols]
        hi = lf.astype(BF16)
        r1 = lf - hi.astype(F32)
        mid = r1.astype(BF16)
        lo = (r1 - mid.astype(F32)).astype(BF16)
        a = (jnp.dot(tri, hi, preferred_element_type=F32)
             + jnp.dot(tri, mid, preferred_element_type=F32)
             + jnp.dot(tri, lo, preferred_element_type=F32))

        st = st_ref[hd]
        o = lax.dot_general((q * jnp.exp(a)).astype(BF16), st.astype(BF16), nt, preferred_element_type=F32)

        att = jnp.zeros((c, c), F32)
        half = c // 2
        while half >= sub:
            pair = 2 * half
            ref = jnp.concatenate(
                [jnp.broadcast_to(a[p * pair + half - 1:p * pair + half, :], (pair, HG_DK))
                 for p in range(c // pair)], axis=0)
            right = (row % pair) >= half
            qt = jnp.where(right, q * jnp.exp(jnp.minimum(a - ref, 0.0)), 0.0)
            kt = jnp.where(right, 0.0, k * jnp.exp(jnp.minimum(ref - a, 0.0)))
            lvl = lax.dot_general(qt.astype(BF16), kt.astype(BF16), nt, preferred_element_type=F32)
            att = lvl if pair == c else att + jnp.where((row // pair) == (col_c // pair), lvl, 0.0)
            half //= 2
        q3 = q.reshape(c // sub, sub, HG_DK)
        k3 = k.reshape(c // sub, sub, HG_DK)
        f3 = f.reshape(c // sub, sub, HG_DK)
        dec = None
        diag = jnp.zeros((c, c), F32)
        for d in range(sub):
            if d == 0:
                prod = q3 * k3
            else:
                f_sh = f3 if d == 1 else pltpu.roll(f3, d - 1, 1)
                dec = f_sh if dec is None else dec * f_sh
                prod = q3 * pltpu.roll(k3, d, 1) * dec
            colv = jnp.sum(prod, axis=2, keepdims=True).reshape(c, 1)
            diag = jnp.where((col_c == row - d) & (row % sub >= d), colv, diag)
        o = o + jnp.dot((att + diag).astype(BF16), v, preferred_element_type=F32)

        a_last = a[c - 1:c, :]
        kd = (k * jnp.exp(a_last - a)).astype(BF16)
        st_ref[hd] = st * jnp.exp(a_last) + lax.dot_general(
            v, kd, (((0,), (0,)), ((), ())), preferred_element_type=F32)

        ms = jnp.mean(o * o, axis=-1, keepdims=True)
        y = o * lax.rsqrt(ms + NORM_EPS) * gain_ref[0, :, cols]
        o_ref[0, pl.ds(r0, c), cols] = (y * _silu(gate_ref[0, pl.ds(r0, c), cols].astype(F32))).astype(BF16)

    def chunk(n, carry):
        r0 = pl.multiple_of(n * c, c)
        for hd in range(nh):
            head_chunk(r0, hd)
        return carry

    lax.fori_loop(0, n_chunks, chunk, 0)


def _hgrn(z16, z32, lb, gain, batch, seq, tc, nh):
    z3 = z16.reshape(batch, seq, Z16_W)
    f3 = z32.reshape(batch, seq, Z32_W)
    w = nh * HG_DK

    def col16(off):
        return pl.BlockSpec((1, tc, w), lambda b, h, i: (b, i, off // w + h))

    return pl.pallas_call(
        functools.partial(_hgrn_kernel, n_chunks=tc // HG_CHUNK, nh=nh),
        grid=(batch, HG_HEADS // nh, seq // tc),
        in_specs=[col16(Z16_CQ),
                  pl.BlockSpec((1, tc, w), lambda b, h, i: (b, i, Z32_CF // w + h)),
                  col16(Z16_CI), col16(Z16_CSILU),
                  pl.BlockSpec((1, 1, w), lambda b, h, i: (h, 0, 0)),
                  pl.BlockSpec((1, 1, w), lambda b, h, i: (h, 0, 0))],
        out_specs=pl.BlockSpec((1, tc, w), lambda b, h, i: (b, i, h)),
        out_shape=jax.ShapeDtypeStruct((batch, seq, D_MODEL), BF16),
        scratch_shapes=[pltpu.VMEM((nh, HG_DK, HG_DK), F32)],
        compiler_params=_cparams(("parallel", "parallel", "arbitrary")),
        name="hgrn",
    )(z3, f3, z3, z3, lb, gain)


def _outproj_kernel(oa_ref, asilu_ref, ob_ref, ocn_ref, m0_ref, m1_ref, m2_ref,
                    x_ref, wb_ref, wo_ref, post_ref, o_ref):
    oa = (oa_ref[...].astype(F32) * _silu(asilu_ref[...].astype(F32))).astype(BF16)
    merged = _sigmoid(m0_ref[...].astype(F32)) * jnp.dot(oa, wb_ref[0], preferred_element_type=F32)
    merged += _sigmoid(m1_ref[...].astype(F32)) * jnp.dot(ob_ref[...], wb_ref[1], preferred_element_type=F32)
    merged += _sigmoid(m2_ref[...].astype(F32)) * jnp.dot(ocn_ref[...], wb_ref[2], preferred_element_type=F32)
    out = jnp.dot(merged.astype(BF16), wo_ref[...], preferred_element_type=F32)
    ms = jnp.mean(out * out, axis=-1, keepdims=True)
    o_ref[...] = x_ref[...] + out * lax.rsqrt(ms + NORM_EPS) * post_ref[...]


def _outproj(oa, z16, ob, ocn, x2, wb, wo, post, tm):
    m = x2.shape[0]

    def rows(j=0):
        return pl.BlockSpec((tm, D_MODEL), lambda i: (i, j))

    return pl.pallas_call(
        _outproj_kernel,
        grid=(m // tm,),
        in_specs=[rows(), rows(Z16_ASILU // D_MODEL), rows(), rows(),
                  rows(Z16_MERGE // D_MODEL), rows(Z16_MERGE // D_MODEL + 1), rows(Z16_MERGE // D_MODEL + 2),
                  rows(),
                  pl.BlockSpec((3, D_MODEL, D_MODEL), lambda i: (0, 0, 0)),
                  pl.BlockSpec((D_MODEL, D_MODEL), lambda i: (0, 0)),
                  pl.BlockSpec((1, D_MODEL), lambda i: (0, 0))],
        out_specs=rows(),
        out_shape=jax.ShapeDtypeStruct((m, D_MODEL), F32),
        compiler_params=_cparams(("parallel",)),
        name="outproj",
    )(oa, z16, ob, ocn, z16, z16, z16, x2, wb, wo, post)


def _split_w_in(w):
    sizes = (1024, 256, 256, 256, 256, 256, 256, 3 * N_G * HPG, 1024, 1024, 1024, 1024, 1024, 1024, 1024, 3072)
    names = ("a_q", "a_kc", "a_vc", "a_ks", "a_vs", "a_kw", "a_vw", "a_gate", "a_silu", "b_v", "b_silu",
             "c_q", "c_f", "c_i", "c_silu", "merge")
    parts, off = {}, 0
    for n, s in zip(names, sizes):
        parts[n] = w[:, off:off + s]
        off += s
    w16 = jnp.concatenate([parts[n] for n in ("a_q", "a_ks", "a_vs", "a_kw", "a_vw", "a_silu", "b_v", "b_silu",
                                              "c_q", "c_i", "c_silu", "merge")], axis=1)
    gate = parts["a_gate"].reshape(D_MODEL, N_G, 3 * HPG)
    gate = jnp.pad(gate, ((0, 0), (0, 0), (0, GATE_PAD - 3 * HPG))).reshape(D_MODEL, N_G * GATE_PAD)
    w32 = jnp.concatenate([parts["c_f"], parts["a_kc"], parts["a_vc"], gate], axis=1)
    return w16.astype(BF16), w32.astype(BF16)


def _block_diag_heads(w):
    n = CMP_LANES // DH
    eye = jnp.eye(n, dtype=w.dtype)
    return jnp.einsum("gh,pde->pgdhe", eye, w).reshape(CMP_BLOCK, CMP_LANES, CMP_LANES)


def _overlap_matrix(seq):
    n_c = seq // CMP_STRIDE
    n_s = seq // SEL_BLOCK
    c_start = jnp.arange(n_c) * CMP_STRIDE
    s_start = jnp.arange(n_s) * SEL_BLOCK
    ov = jnp.clip(jnp.minimum(c_start[None, :] + CMP_BLOCK, s_start[:, None] + SEL_BLOCK)
                  - jnp.maximum(c_start[None, :], s_start[:, None]), 0).astype(F32) / CMP_BLOCK
    return ov.astype(BF16)


def kernel(x, pre_norm, w_in, cmp_pe_k, cmp_pe_v, cmp_w_k, cmp_w_v, pool_w, pool_scale, hgrn_lb_logits, hgrn_norm,
           w_branch, w_out, post_norm):
    batch, seq, _ = x.shape
    depth = w_in.shape[0]
    m = batch * seq
    probs = jax.nn.softmax(hgrn_lb_logits.astype(F32), axis=0)
    lbs = jnp.cumsum(probs, axis=0) - probs[0]
    ov = _overlap_matrix(seq)
    tm_in = min(2048, m)
    tt = min(512, seq)

    x2 = x.reshape(m, D_MODEL)
    for l in range(depth):
        w16, w32 = _split_w_in(w_in[l])
        gain = pre_norm[l].reshape(1, D_MODEL)
        z16 = _inproj(x2, gain, w16, BF16, tm_in, 1024)
        z32 = _inproj(x2, gain, w32, F32, tm_in, 1024)

        pe = jnp.stack([jnp.tile(cmp_pe_k[l], (1, CMP_LANES // DH)), jnp.tile(cmp_pe_v[l], (1, CMP_LANES // DH))])
        wbd = jnp.stack([_block_diag_heads(cmp_w_k[l]), _block_diag_heads(cmp_w_v[l])]).astype(BF16)
        ckv, ckvt = _compress(z32, pe, wbd, batch, seq)
        oc, qaug = _cmp_select(z16, z32, ckv, ckvt, ov, batch, seq)
        kaug, vaug, kwn, vwaug = _kv_prep(z16, batch, seq, tt)
        oa = _nsa_attn(qaug, z32, oc, kaug, vaug, kwn, vwaug, batch, seq, min(SEL_TK, seq // 2))

        ob = _pool(z16, pool_w[l].astype(BF16), pool_scale[l].reshape(1, D_MODEL), batch, seq, tt)
        ocn = _hgrn(z16, z32, lbs[l].reshape(HG_HEADS // HG_NH, 1, HG_NH * HG_DK),
                    hgrn_norm[l].reshape(HG_HEADS // HG_NH, 1, HG_NH * HG_DK), batch, seq, tt, HG_NH)

        x2 = _outproj(oa.reshape(m, D_MODEL), z16, ob.reshape(m, D_MODEL), ocn.reshape(m, D_MODEL), x2,
                      w_branch[l].astype(BF16), w_out[l].astype(BF16), post_norm[l].reshape(1, D_MODEL), 256)
    return x2.reshape(batch, seq, D_MODEL)
```

```python
import functools

import jax
import jax.numpy as jnp
from jax import lax
from jax.experimental import pallas as pl
from jax.experimental.pallas import tpu as pltpu

F32 = jnp.float32
BF16 = jnp.bfloat16

D_MODEL = 1024
NORM_EPS = 1e-6
DH = 64
N_G = 4
HPG = 4
GW = HPG * DH
KVW = N_G * DH
CMP_BLOCK = 32
CMP_STRIDE = 16
CMP_LANES = 128
SEL_BLOCK = 64
N_SELECT = 16
WINDOW = 512
QT = 256
SEL_TK = 512
CQ = 256
CMP_ROW_LEVELS = 4
LANES = 128
FORCE_SCORE = 1.0e4
ATT_SCALE = DH ** -0.5
MASK_BIAS = -(2.0 ** 30)
LOG2E = 1.4426950408889634
POOL_WINDOWS = (2, 4, 8, 16)
POOL_GROUP = D_MODEL // len(POOL_WINDOWS)
POOL_HALO = 16
HG_HEADS = 8
HG_DK = 128
HG_CHUNK = 64
HG_SUB = 8
HG_NH = 8

Z16_Q, Z16_KS, Z16_VS, Z16_KW, Z16_VW = 0, 1024, 1280, 1536, 1792
Z16_ASILU, Z16_BV, Z16_BSILU = 2048, 3072, 4096
Z16_CQ, Z16_CI, Z16_CSILU, Z16_MERGE = 5120, 6144, 7168, 8192
Z16_W = 11264
Z32_CF, Z32_KC, Z32_VC, Z32_GATE = 0, 1024, 1280, 1536
Z32_W = 2048
GATE_PAD = 128

VMEM_LIMIT = 56 * 1024 * 1024


def _cparams(sem):
    return pltpu.CompilerParams(dimension_semantics=sem, vmem_limit_bytes=VMEM_LIMIT)


def _sigmoid(x):
    return 1.0 / (1.0 + jnp.exp(-x))


def _silu(x):
    return x * _sigmoid(x)


def _inproj_kernel(x_ref, g_ref, w_ref, o_ref, h_ref):
    @pl.when(pl.program_id(1) == 0)
    def _():
        x = x_ref[...]
        ms = jnp.mean(x * x, axis=-1, keepdims=True)
        h_ref[...] = (x * lax.rsqrt(ms + NORM_EPS) * g_ref[...]).astype(BF16)

    o_ref[...] = jnp.dot(h_ref[...], w_ref[...], preferred_element_type=F32).astype(o_ref.dtype)


def _inproj(x2, gain, w, out_dtype, tm, tn):
    m, d = x2.shape
    n = w.shape[1]
    return pl.pallas_call(
        _inproj_kernel,
        grid=(m // tm, n // tn),
        in_specs=[pl.BlockSpec((tm, d), lambda i, j: (i, 0)),
                  pl.BlockSpec((1, d), lambda i, j: (0, 0)),
                  pl.BlockSpec((d, tn), lambda i, j: (0, j))],
        out_specs=pl.BlockSpec((tm, tn), lambda i, j: (i, j)),
        out_shape=jax.ShapeDtypeStruct((m, n), out_dtype),
        scratch_shapes=[pltpu.VMEM((tm, d), BF16)],
        compiler_params=_cparams(("parallel", "arbitrary")),
        name="inproj",
    )(x2, gain, w)


def _compress_kernel(x_ref, pe_ref, w_ref, o_ref, ot_ref, *, nch):
    acc_a = jnp.zeros((nch, CMP_LANES), F32)
    acc_b = jnp.zeros((nch, CMP_LANES), F32)
    for p in range(CMP_STRIDE):
        xp = x_ref[0, pl.ds(p, nch, stride=CMP_STRIDE), :]
        a = (xp + pe_ref[0, p:p + 1, :]).astype(BF16)
        b = (xp + pe_ref[0, CMP_STRIDE + p:CMP_STRIDE + p + 1, :]).astype(BF16)
        acc_a += jnp.dot(a, w_ref[0, p], preferred_element_type=F32)
        acc_b += jnp.dot(b, w_ref[0, CMP_STRIDE + p], preferred_element_type=F32)
    c = acc_a + pltpu.roll(acc_b, nch - 1, 0)
    ct = c.T
    ck = c * LOG2E
    for g in range(CMP_LANES // DH):
        o_ref[0, 0, g] = ck[:, g * DH:(g + 1) * DH].astype(BF16)
        ot_ref[0, 0, g] = ct[g * DH:(g + 1) * DH, :].astype(BF16)


def _compress(z32, pe, wbd, batch, seq):
    nch = seq // CMP_STRIDE
    z3 = z32.reshape(batch, seq, Z32_W)
    gpb = CMP_LANES // DH
    return pl.pallas_call(
        functools.partial(_compress_kernel, nch=nch),
        grid=(batch, 2, N_G // gpb),
        in_specs=[pl.BlockSpec((1, seq, CMP_LANES),
                               lambda b, s, c: (b, 0, (Z32_KC + s * KVW) // CMP_LANES + c)),
                  pl.BlockSpec((1, CMP_BLOCK, CMP_LANES), lambda b, s, c: (s, 0, 0)),
                  pl.BlockSpec((1, CMP_BLOCK, CMP_LANES, CMP_LANES), lambda b, s, c: (s, 0, 0, 0))],
        out_specs=[pl.BlockSpec((1, 1, gpb, nch, DH), lambda b, s, c: (b, s, c, 0, 0)),
                   pl.BlockSpec((1, 1, gpb, DH, nch), lambda b, s, c: (b, s, c, 0, 0))],
        out_shape=[jax.ShapeDtypeStruct((batch, 2, N_G, nch, DH), BF16),
                   jax.ShapeDtypeStruct((batch, 2, N_G, DH, nch), BF16)],
        compiler_params=_cparams(("parallel", "parallel", "parallel")),
        name="compress",
    )(z3, pe, wbd)


def _cmp_select_kernel(q_ref, ck_ref, cvt_ref, ov_ref, gate_ref, oc_ref, qa_ref, imp_ref, *, nch, n_s, n_sel):
    s0 = pl.program_id(2) * CQ
    qt = (q_ref[0].astype(F32).T * ATT_SCALE).astype(BF16)
    t_lane = s0 + lax.broadcasted_iota(jnp.int32, (1, CQ), 1)
    n_vis = (s0 + CQ - CMP_BLOCK) // CMP_STRIDE + 1

    def attend(rows):
        ck = ck_ref[0, 0, 0, 0:rows, :]
        cvt = cvt_ref[0, 0, 0, :, 0:rows]
        c_end = lax.broadcasted_iota(jnp.int32, (rows, 1), 0) * CMP_STRIDE + (CMP_BLOCK - 1)
        visible = c_end <= t_lane
        gates = _sigmoid(gate_ref[0])
        p_sum = jnp.zeros((rows, CQ), F32)
        outs = []
        for h in range(HPG):
            s = jnp.dot(ck, qt[h * DH:(h + 1) * DH, :], preferred_element_type=F32)
            s = jnp.where(visible, s, -jnp.inf)
            m = jnp.max(s, axis=0, keepdims=True)
            m = jnp.where(m == -jnp.inf, 0.0, m)
            e = jnp.exp2(s - m)
            p = e / jnp.maximum(jnp.sum(e, axis=0, keepdims=True), 1e-30)
            p_sum += p
            oct_h = jnp.dot(cvt, p.astype(BF16), preferred_element_type=F32)
            outs.append(oct_h.T * gates[:, 3 * h:3 * h + 1])
        oc_ref[0] = jnp.concatenate(outs, axis=1).astype(BF16)
        p_hi = p_sum.astype(BF16)
        p_lo = (p_sum - p_hi.astype(F32)).astype(BF16)
        ov = ov_ref[:, 0:rows]
        imp_ref[...] = (jnp.dot(ov, p_hi, preferred_element_type=F32)
                        + jnp.dot(ov, p_lo, preferred_element_type=F32))

    quantum = nch // CMP_ROW_LEVELS
    for j in range(CMP_ROW_LEVELS):
        rows = quantum * (j + 1)
        if j == 0:
            cond = n_vis <= rows
        else:
            cond = (n_vis > rows - quantum) & (n_vis <= rows)
        pl.when(cond)(functools.partial(attend, rows))

    blk = lax.broadcasted_iota(jnp.int32, (n_s, 1), 0)

    def pick(_, sc):
        m = jnp.max(sc, axis=0, keepdims=True)
        first = jnp.min(jnp.where(sc == m, blk, n_s), axis=0, keepdims=True)
        return jnp.where(blk == first, -jnp.inf, sc)

    biases = []
    for j in range(CQ // LANES):
        lanes = slice(j * LANES, (j + 1) * LANES)
        cur = t_lane[:, lanes] // SEL_BLOCK
        valid = blk <= cur
        force = (blk == 0) | (blk == cur) | (blk == cur - 1)
        score = jnp.where(valid, jnp.where(force, FORCE_SCORE, imp_ref[:, lanes]), -1.0)
        sc = lax.fori_loop(0, n_sel, pick, score, unroll=True)
        biases.append(jnp.where((sc == -jnp.inf) & valid, 0.0, MASK_BIAS).astype(BF16))
    bias_t = jnp.concatenate(biases, axis=1)
    for j in range(CQ // QT):
        lanes = slice(j * QT, (j + 1) * QT)
        qa_ref[0, 0, j, 0:n_s, :] = jnp.concatenate([bias_t[:, lanes]] * HPG, axis=1)
        qa_ref[0, 0, j, n_s:n_s + DH, :] = jnp.concatenate(
            [qt[h * DH:(h + 1) * DH, lanes] for h in range(HPG)], axis=1)


def _cmp_select(z16, z32, ck, cvt, ov, batch, seq):
    nch = seq // CMP_STRIDE
    n_s = seq // SEL_BLOCK
    n_sel = min(N_SELECT, n_s)
    z3 = z16.reshape(batch, seq, Z16_W)
    g3 = z32.reshape(batch, seq, Z32_W)
    return pl.pallas_call(
        functools.partial(_cmp_select_kernel, nch=nch, n_s=n_s, n_sel=n_sel),
        grid=(batch, N_G, seq // CQ),
        in_specs=[pl.BlockSpec((1, CQ, GW), lambda b, g, i: (b, i, Z16_Q // GW + g)),
                  pl.BlockSpec((1, 1, 1, nch, DH), lambda b, g, i: (b, 0, g, 0, 0)),
                  pl.BlockSpec((1, 1, 1, DH, nch), lambda b, g, i: (b, 1, g, 0, 0)),
                  pl.BlockSpec((n_s, nch), lambda b, g, i: (0, 0)),
                  pl.BlockSpec((1, CQ, GATE_PAD), lambda b, g, i: (b, i, Z32_GATE // GATE_PAD + g))],
        out_specs=[pl.BlockSpec((1, CQ, GW), lambda b, g, i: (b, i, g)),
                   pl.BlockSpec((1, 1, CQ // QT, n_s + DH, HPG * QT), lambda b, g, i: (b, g, i, 0, 0))],
        out_shape=[jax.ShapeDtypeStruct((batch, seq, D_MODEL), BF16),
                   jax.ShapeDtypeStruct((batch, N_G, seq // QT, n_s + DH, HPG * QT), BF16)],
        scratch_shapes=[pltpu.VMEM((n_s, CQ), F32)],
        compiler_params=_cparams(("parallel", "parallel", "parallel")),
        name="cmp_select",
    )(z3, ck, cvt, ov, g3)


def _kv_prep_kernel(ks_ref, vs_ref, kw_ref, vw_ref, ka_ref, va_ref, kwn_ref, vwa_ref, *, n_s, tt):
    t0 = pl.program_id(1) * tt
    ks = ks_ref[0]
    kw = kw_ref[0]
    vst = vs_ref[0].astype(F32).T
    vwt = vw_ref[0].astype(F32).T
    pos = t0 + lax.broadcasted_iota(jnp.int32, (tt, 1), 0)
    blk = lax.broadcasted_iota(jnp.int32, (1, n_s), 1)
    onehot = jnp.where(pos // SEL_BLOCK == blk, 1.0, 0.0).astype(BF16)
    ones_row = jnp.where(lax.broadcasted_iota(jnp.int32, (DH, tt), 0) == 0, 1.0, 0.0).astype(BF16)
    for g in range(N_G):
        ka_ref[0, g, :, 0:n_s] = onehot
        ka_ref[0, g, :, n_s:n_s + DH] = ks[:, g * DH:(g + 1) * DH]
        kwn_ref[0, g] = kw[:, g * DH:(g + 1) * DH]
        va_ref[0, g, 0:DH, :] = vst[g * DH:(g + 1) * DH, :].astype(BF16)
        va_ref[0, g, DH:2 * DH, :] = ones_row
        vwa_ref[0, g, 0:DH, :] = vwt[g * DH:(g + 1) * DH, :].astype(BF16)
        vwa_ref[0, g, DH:2 * DH, :] = ones_row


def _kv_prep(z16, batch, seq, tt):
    n_s = seq // SEL_BLOCK
    z3 = z16.reshape(batch, seq, Z16_W)

    def col(off):
        return pl.BlockSpec((1, tt, KVW), lambda b, i: (b, i, off // KVW))

    return pl.pallas_call(
        functools.partial(_kv_prep_kernel, n_s=n_s, tt=tt),
        grid=(batch, seq // tt),
        in_specs=[col(Z16_KS), col(Z16_VS), col(Z16_KW), col(Z16_VW)],
        out_specs=[pl.BlockSpec((1, N_G, tt, n_s + DH), lambda b, i: (b, 0, i, 0)),
                   pl.BlockSpec((1, N_G, 2 * DH, tt), lambda b, i: (b, 0, 0, i)),
                   pl.BlockSpec((1, N_G, tt, DH), lambda b, i: (b, 0, i, 0)),
                   pl.BlockSpec((1, N_G, 2 * DH, tt), lambda b, i: (b, 0, 0, i))],
        out_shape=[jax.ShapeDtypeStruct((batch, N_G, seq, n_s + DH), BF16),
                   jax.ShapeDtypeStruct((batch, N_G, 2 * DH, seq), BF16),
                   jax.ShapeDtypeStruct((batch, N_G, seq, DH), BF16),
                   jax.ShapeDtypeStruct((batch, N_G, 2 * DH, seq), BF16)],
        compiler_params=_cparams(("parallel", "parallel")),
        name="kv_prep",
    )(z3, z3, z3, z3)


def _nsa_attn_kernel(qa_ref, gate_ref, oc_ref, k_ref, v_ref, kw_ref, vw_ref, o_ref,
                     sa_ref, sb_ref, ma_ref, mb_ref, m_ref, acc_ref, accw_ref, *, tk, n_s, span):
    s0 = pl.program_id(2) * QT
    unit = 2 * tk
    n_units = s0 // unit
    u0 = pl.multiple_of(n_units * unit, unit)
    m_ref[...] = jnp.full(m_ref.shape, -jnp.inf, F32)
    acc_ref[...] = jnp.zeros(acc_ref.shape, F32)
    t_lane = s0 + lax.broadcasted_iota(jnp.int32, (1, HPG * QT), 1) % QT

    start = pl.multiple_of(jnp.maximum(s0 - WINDOW, 0), QT)
    sw = jnp.dot(kw_ref[0, 0, pl.ds(start, span), :], qa_ref[0, 0, 0, n_s:n_s + DH, :],
                 preferred_element_type=F32)
    kpos_w = start + lax.broadcasted_iota(jnp.int32, (span, 1), 0)
    sw = jnp.where((kpos_w <= t_lane) & (t_lane - kpos_w < WINDOW), sw, -jnp.inf)
    pw = jnp.exp2(sw - jnp.max(sw, axis=0, keepdims=True)).astype(BF16)
    accw_ref[...] = jnp.dot(vw_ref[0, 0, :, pl.ds(start, span)], pw, preferred_element_type=F32)

    def scores(off):
        return jnp.dot(k_ref[0, 0, pl.ds(off, tk), :], qa_ref[0, 0, 0], preferred_element_type=F32)

    def causal(s, off):
        kpos = off + lax.broadcasted_iota(jnp.int32, (tk, 1), 0)
        return jnp.where(kpos <= t_lane, s, MASK_BIAS)

    def stage(s_ref, mx_ref, off):
        s = scores(off)
        s_ref[...] = s
        mx_ref[...] = jnp.max(s, axis=0, keepdims=True)

    def accumulate(s, tile_max, off):
        m_prev = m_ref[...]
        m_new = jnp.maximum(m_prev, tile_max)
        p = jnp.exp2(s - m_new).astype(BF16)
        acc_ref[...] = jnp.exp2(m_prev - m_new) * acc_ref[...] + jnp.dot(
            v_ref[0, 0, :, pl.ds(off, tk)], p, preferred_element_type=F32)
        m_ref[...] = m_new

    def accumulate_causal(off):
        s = causal(scores(off), off)
        accumulate(s, jnp.max(s, axis=0, keepdims=True), off)

    stage(sa_ref, ma_ref, 0)

    accumulate_causal(u0)

    @pl.when(s0 >= u0 + tk)
    def _():
        accumulate_causal(pl.multiple_of(u0 + tk, tk))

    @pl.when(n_units > 0)
    def _():
        def body(u, carry):
            off = pl.multiple_of(u * unit, unit)
            stage(sb_ref, mb_ref, off + tk)
            accumulate(sa_ref[...], ma_ref[...], off)
            nxt = pl.multiple_of(jnp.minimum(u + 1, n_units - 1) * unit, unit)
            stage(sa_ref, ma_ref, nxt)
            accumulate(sb_ref[...], mb_ref[...], off + tk)
            return carry

        lax.fori_loop(0, n_units, body, 0)

    gates_t = _sigmoid(gate_ref[0]).T
    outs = []
    for h in range(HPG):
        a = acc_ref[:, h * QT:(h + 1) * QT]
        aw = accw_ref[:, h * QT:(h + 1) * QT]
        w_sel = gates_t[3 * h + 1:3 * h + 2, :] / a[DH:DH + 1, :]
        w_win = gates_t[3 * h + 2:3 * h + 3, :] / aw[DH:DH + 1, :]
        outs.append((a[0:DH, :] * w_sel + aw[0:DH, :] * w_win).T)
    o_ref[0] = (oc_ref[0].astype(F32) + jnp.concatenate(outs, axis=1)).astype(BF16)


def _nsa_attn(qaug, z32, oc, kaug, vaug, kwn, vwaug, batch, seq, tk):
    kd = kaug.shape[-1]
    span = min(WINDOW + QT, seq)
    g3 = z32.reshape(batch, seq, Z32_W)

    def per_head(shape):
        return pl.BlockSpec((1, 1) + shape, lambda b, g, i: (b, g, 0, 0))

    return pl.pallas_call(
        functools.partial(_nsa_attn_kernel, tk=tk, n_s=kd - DH, span=span),
        grid=(batch, N_G, seq // QT),
        in_specs=[pl.BlockSpec((1, 1, 1, kd, HPG * QT), lambda b, g, i: (b, g, i, 0, 0)),
                  pl.BlockSpec((1, QT, GATE_PAD), lambda b, g, i: (b, i, Z32_GATE // GATE_PAD + g)),
                  pl.BlockSpec((1, QT, GW), lambda b, g, i: (b, i, g)),
                  per_head((seq, kd)), per_head((2 * DH, seq)), per_head((seq, DH)), per_head((2 * DH, seq))],
        out_specs=pl.BlockSpec((1, QT, GW), lambda b, g, i: (b, i, g)),
        out_shape=jax.ShapeDtypeStruct((batch, seq, D_MODEL), BF16),
        scratch_shapes=[pltpu.VMEM((tk, HPG * QT), F32),
                        pltpu.VMEM((tk, HPG * QT), F32),
                        pltpu.VMEM((1, HPG * QT), F32),
                        pltpu.VMEM((1, HPG * QT), F32),
                        pltpu.VMEM((1, HPG * QT), F32),
                        pltpu.VMEM((2 * DH, HPG * QT), F32),
                        pltpu.VMEM((2 * DH, HPG * QT), F32)],
        compiler_params=_cparams(("parallel", "parallel", "arbitrary")),
        name="nsa_attn",
    )(qaug, g3, oc, kaug, vaug, kwn, vwaug)


def _pool_kernel(v_ref, prev_ref, gate_ref, w_ref, scale_ref, o_ref, *, tt):
    i = pl.program_id(1)
    cur = v_ref[0].astype(F32)
    prev = jnp.where(i > 0, prev_ref[0].astype(F32), 0.0)
    t = i * tt + lax.broadcasted_iota(jnp.int32, (tt, 1), 0)
    for g, w in enumerate(POOL_WINDOWS):
        sl = slice(g * POOL_GROUP, (g + 1) * POOL_GROUP)
        ext = jnp.concatenate([prev[:, sl], cur[:, sl]], axis=0)
        acc = ext
        d = 1
        while d < w:
            acc = acc + pltpu.roll(acc, d, 0)
            d *= 2
        cnt = jnp.minimum(t + 1, w).astype(F32)
        pooled = acc[POOL_HALO:, :] / cnt - cur[:, sl]
        mixed = jnp.dot(pooled.astype(BF16), w_ref[g], preferred_element_type=F32)
        o_ref[0, :, sl] = (mixed * scale_ref[:, sl] * _silu(gate_ref[0, :, sl].astype(F32))).astype(BF16)


def _pool(z16, w_pool, scale, batch, seq, tt):
    z3 = z16.reshape(batch, seq, Z16_W)
    hb = tt // POOL_HALO
    return pl.pallas_call(
        functools.partial(_pool_kernel, tt=tt),
        grid=(batch, seq // tt),
        in_specs=[pl.BlockSpec((1, tt, D_MODEL), lambda b, i: (b, i, Z16_BV // D_MODEL)),
                  pl.BlockSpec((1, POOL_HALO, D_MODEL),
                               lambda b, i: (b, jnp.maximum(i * hb - 1, 0), Z16_BV // D_MODEL)),
                  pl.BlockSpec((1, tt, D_MODEL), lambda b, i: (b, i, Z16_BSILU // D_MODEL)),
                  pl.BlockSpec((len(POOL_WINDOWS), POOL_GROUP, POOL_GROUP), lambda b, i: (0, 0, 0)),
                  pl.BlockSpec((1, D_MODEL), lambda b, i: (0, 0))],
        out_specs=pl.BlockSpec((1, tt, D_MODEL), lambda b, i: (b, i, 0)),
        out_shape=jax.ShapeDtypeStruct((batch, seq, D_MODEL), BF16),
        compiler_params=_cparams(("parallel", "parallel")),
        name="pool",
    )(z3, z3, z3, w_pool, scale)


def _hgrn_kernel(q_ref, f_ref, i_ref, gate_ref, lb_ref, gain_ref, o_ref, st_ref, *, n_chunks, nh):
    c, sub = HG_CHUNK, HG_SUB

    @pl.when(pl.program_id(2) == 0)
    def _():
        st_ref[...] = jnp.zeros(st_ref.shape, F32)

    row = lax.broadcasted_iota(jnp.int32, (c, 1), 0)
    tri = jnp.where(lax.broadcasted_iota(jnp.int32, (c, c), 1) <= row, 1.0, 0.0).astype(BF16)
    col_c = lax.broadcasted_iota(jnp.int32, (1, c), 1)
    nt = (((1,), (1,)), ((), ()))

    def head_chunk(r0, hd):
        cols = slice(hd * HG_DK, (hd + 1) * HG_DK)
        lb = lb_ref[0, :, cols]
        f = lb + (1.0 - lb) * _sigmoid(f_ref[0, pl.ds(r0, c), cols])
        lf = jnp.log2(f)
        k = 1.0 - f
        q = q_ref[0, pl.ds(r0, c), cols].astype(F32)
        v = i_ref[0, pl.ds(r0, c), cols]
        hi = lf.astype(BF16)
        r1 = lf - hi.astype(F32)
        mid = r1.astype(BF16)
        lo = (r1 - mid.astype(F32)).astype(BF16)
        a = (jnp.dot(tri, hi, preferred_element_type=F32)
             + jnp.dot(tri, mid, preferred_element_type=F32)
             + jnp.dot(tri, lo, preferred_element_type=F32))

        st = st_ref[hd]
        o = lax.dot_general((q * jnp.exp2(a)).astype(BF16), st.astype(BF16), nt, preferred_element_type=F32)

        att = jnp.zeros((c, c), F32)
        half = c // 2
        while half >= sub:
            pair = 2 * half
            ref = jnp.concatenate(
                [jnp.broadcast_to(a[p * pair + half - 1:p * pair + half, :], (pair, HG_DK))
                 for p in range(c // pair)], axis=0)
            right = (row % pair) >= half
            qt = jnp.where(right, q * jnp.exp2(jnp.minimum(a - ref, 0.0)), 0.0)
            kt = jnp.where(right, 0.0, k * jnp.exp2(jnp.minimum(ref - a, 0.0)))
            lvl = lax.dot_general(qt.astype(BF16), kt.astype(BF16), nt, preferred_element_type=F32)
            att = lvl if pair == c else att + jnp.where((row // pair) == (col_c // pair), lvl, 0.0)
            half //= 2
        q3 = q.reshape(c // sub, sub, HG_DK)
        k3 = k.reshape(c // sub, sub, HG_DK)
        f3 = f.reshape(c // sub, sub, HG_DK)
        dec = None
        diag = jnp.zeros((c, c), F32)
        for d in range(sub):
            if d == 0:
                prod = q3 * k3
            else:
                f_sh = f3 if d == 1 else pltpu.roll(f3, d - 1, 1)
                dec = f_sh if dec is None else dec * f_sh
                prod = q3 * pltpu.roll(k3, d, 1) * dec
            colv = jnp.sum(prod, axis=2, keepdims=True).reshape(c, 1)
            diag = jnp.where((col_c == row - d) & (row % sub >= d), colv, diag)
        o = o + jnp.dot((att + diag).astype(BF16), v, preferred_element_type=F32)

        a_last = a[c - 1:c, :]
        kd = (k * jnp.exp2(a_last - a)).astype(BF16)
        st_ref[hd] = st * jnp.exp2(a_last) + lax.dot_general(
            v, kd, (((0,), (0,)), ((), ())), preferred_element_type=F32)

        ms = jnp.mean(o * o, axis=-1, keepdims=True)
        y = o * lax.rsqrt(ms + NORM_EPS) * gain_ref[0, :, cols]
        o_ref[0, pl.ds(r0, c), cols] = (y * _silu(gate_ref[0, pl.ds(r0, c), cols].astype(F32))).astype(BF16)

    def chunk(n, carry):
        r0 = pl.multiple_of(n * c, c)
        for hd in range(nh):
            head_chunk(r0, hd)
        return carry

    lax.fori_loop(0, n_chunks, chunk, 0)


def _hgrn(z16, z32, lb, gain, batch, seq, tc, nh):
    z3 = z16.reshape(batch, seq, Z16_W)
    f3 = z32.reshape(batch, seq, Z32_W)
    w = nh * HG_DK

    def col16(off):
        return pl.BlockSpec((1, tc, w), lambda b, h, i: (b, i, off // w + h))

    return pl.pallas_call(
        functools.partial(_hgrn_kernel, n_chunks=tc // HG_CHUNK, nh=nh),
        grid=(batch, HG_HEADS // nh, seq // tc),
        in_specs=[col16(Z16_CQ),
                  pl.BlockSpec((1, tc, w), lambda b, h, i: (b, i, Z32_CF // w + h)),
                  col16(Z16_CI), col16(Z16_CSILU),
                  pl.BlockSpec((1, 1, w), lambda b, h, i: (h, 0, 0)),
                  pl.BlockSpec((1, 1, w), lambda b, h, i: (h, 0, 0))],
        out_specs=pl.BlockSpec((1, tc, w), lambda b, h, i: (b, i, h)),
        out_shape=jax.ShapeDtypeStruct((batch, seq, D_MODEL), BF16),
        scratch_shapes=[pltpu.VMEM((nh, HG_DK, HG_DK), F32)],
        compiler_params=_cparams(("parallel", "parallel", "arbitrary")),
        name="hgrn",
    )(z3, f3, z3, z3, lb, gain)


def _outproj_kernel(oa_ref, asilu_ref, ob_ref, ocn_ref, m0_ref, m1_ref, m2_ref,
                    x_ref, wb_ref, wo_ref, post_ref, o_ref):
    oa = (oa_ref[...].astype(F32) * _silu(asilu_ref[...].astype(F32))).astype(BF16)
    merged = _sigmoid(m0_ref[...].astype(F32)) * jnp.dot(oa, wb_ref[0], preferred_element_type=F32)
    merged += _sigmoid(m1_ref[...].astype(F32)) * jnp.dot(ob_ref[...], wb_ref[1], preferred_element_type=F32)
    merged += _sigmoid(m2_ref[...].astype(F32)) * jnp.dot(ocn_ref[...], wb_ref[2], preferred_element_type=F32)
    out = jnp.dot(merged.astype(BF16), wo_ref[...], preferred_element_type=F32)
    ms = jnp.mean(out * out, axis=-1, keepdims=True)
    o_ref[...] = x_ref[...] + out * lax.rsqrt(ms + NORM_EPS) * post_ref[...]


def _outproj(oa, z16, ob, ocn, x2, wb, wo, post, tm):
    m = x2.shape[0]

    def rows(j=0):
        return pl.BlockSpec((tm, D_MODEL), lambda i: (i, j))

    return pl.pallas_call(
        _outproj_kernel,
        grid=(m // tm,),
        in_specs=[rows(), rows(Z16_ASILU // D_MODEL), rows(), rows(),
                  rows(Z16_MERGE // D_MODEL), rows(Z16_MERGE // D_MODEL + 1), rows(Z16_MERGE // D_MODEL + 2),
                  rows(),
                  pl.BlockSpec((3, D_MODEL, D_MODEL), lambda i: (0, 0, 0)),
                  pl.BlockSpec((D_MODEL, D_MODEL), lambda i: (0, 0)),
                  pl.BlockSpec((1, D_MODEL), lambda i: (0, 0))],
        out_specs=rows(),
        out_shape=jax.ShapeDtypeStruct((m, D_MODEL), F32),
        compiler_params=_cparams(("parallel",)),
        name="outproj",
    )(oa, z16, ob, ocn, z16, z16, z16, x2, wb, wo, post)


def _split_w_in(w):
    sizes = (1024, 256, 256, 256, 256, 256, 256, 3 * N_G * HPG, 1024, 1024, 1024, 1024, 1024, 1024, 1024, 3072)
    names = ("a_q", "a_kc", "a_vc", "a_ks", "a_vs", "a_kw", "a_vw", "a_gate", "a_silu", "b_v", "b_silu",
             "c_q", "c_f", "c_i", "c_silu", "merge")
    parts, off = {}, 0
    for n, s in zip(names, sizes):
        parts[n] = w[:, off:off + s]
        off += s
    parts["a_ks"] = parts["a_ks"] * LOG2E
    parts["a_kw"] = parts["a_kw"] * LOG2E
    w16 = jnp.concatenate([parts[n] for n in ("a_q", "a_ks", "a_vs", "a_kw", "a_vw", "a_silu", "b_v", "b_silu",
                                              "c_q", "c_i", "c_silu", "merge")], axis=1)
    gate = parts["a_gate"].reshape(D_MODEL, N_G, 3 * HPG)
    gate = jnp.pad(gate, ((0, 0), (0, 0), (0, GATE_PAD - 3 * HPG))).reshape(D_MODEL, N_G * GATE_PAD)
    w32 = jnp.concatenate([parts["c_f"], parts["a_kc"], parts["a_vc"], gate], axis=1)
    return w16.astype(BF16), w32.astype(BF16)


def _block_diag_heads(w):
    n = CMP_LANES // DH
    eye = jnp.eye(n, dtype=w.dtype)
    return jnp.einsum("gh,pde->pgdhe", eye, w).reshape(CMP_BLOCK, CMP_LANES, CMP_LANES)


def _overlap_matrix(seq):
    n_c = seq // CMP_STRIDE
    n_s = seq // SEL_BLOCK
    c_start = jnp.arange(n_c) * CMP_STRIDE
    s_start = jnp.arange(n_s) * SEL_BLOCK
    ov = jnp.clip(jnp.minimum(c_start[None, :] + CMP_BLOCK, s_start[:, None] + SEL_BLOCK)
                  - jnp.maximum(c_start[None, :], s_start[:, None]), 0).astype(F32) / CMP_BLOCK
    return ov.astype(BF16)


def kernel(x, pre_norm, w_in, cmp_pe_k, cmp_pe_v, cmp_w_k, cmp_w_v, pool_w, pool_scale, hgrn_lb_logits, hgrn_norm,
           w_branch, w_out, post_norm):
    batch, seq, _ = x.shape
    depth = w_in.shape[0]
    m = batch * seq
    probs = jax.nn.softmax(hgrn_lb_logits.astype(F32), axis=0)
    lbs = jnp.cumsum(probs, axis=0) - probs[0]
    ov = _overlap_matrix(seq)
    tm_in = min(2048, m)
    tt = min(512, seq)

    x2 = x.reshape(m, D_MODEL)
    for l in range(depth):
        w16, w32 = _split_w_in(w_in[l])
        gain = pre_norm[l].reshape(1, D_MODEL)
        z16 = _inproj(x2, gain, w16, BF16, tm_in, 1024)
        z32 = _inproj(x2, gain, w32, F32, tm_in, 1024)

        pe = jnp.stack([jnp.tile(cmp_pe_k[l], (1, CMP_LANES // DH)), jnp.tile(cmp_pe_v[l], (1, CMP_LANES // DH))])
        wbd = jnp.stack([_block_diag_heads(cmp_w_k[l]), _block_diag_heads(cmp_w_v[l])]).astype(BF16)
        ckv, ckvt = _compress(z32, pe, wbd, batch, seq)
        oc, qaug = _cmp_select(z16, z32, ckv, ckvt, ov, batch, seq)
        kaug, vaug, kwn, vwaug = _kv_prep(z16, batch, seq, tt)
        oa = _nsa_attn(qaug, z32, oc, kaug, vaug, kwn, vwaug, batch, seq, min(SEL_TK, seq // 2))

        ob = _pool(z16, pool_w[l].astype(BF16), pool_scale[l].reshape(1, D_MODEL), batch, seq, tt)
        ocn = _hgrn(z16, z32, lbs[l].reshape(HG_HEADS // HG_NH, 1, HG_NH * HG_DK),
                    hgrn_norm[l].reshape(HG_HEADS // HG_NH, 1, HG_NH * HG_DK), batch, seq, tt, HG_NH)

        x2 = _outproj(oa.reshape(m, D_MODEL), z16, ob.reshape(m, D_MODEL), ocn.reshape(m, D_MODEL), x2,
                      w_branch[l].astype(BF16), w_out[l].astype(BF16), post_norm[l].reshape(1, D_MODEL), 256)
    return x2.reshape(batch, seq, D_MODEL)
```

```python
import functools

import jax
import jax.numpy as jnp
from jax import lax
from jax.experimental import pallas as pl
from jax.experimental.pallas import tpu as pltpu

F32 = jnp.float32
BF16 = jnp.bfloat16

D_MODEL = 1024
NORM_EPS = 1e-6
DH = 64
N_G = 4
HPG = 4
GW = HPG * DH
KVW = N_G * DH
CMP_BLOCK = 32
CMP_STRIDE = 16
CMP_LANES = 128
SEL_BLOCK = 64
N_SELECT = 16
WINDOW = 512
QT = 512
SEL_TK = 512
CQ = 512
CMP_ROW_LEVELS = 4
LANES = 128
FORCE_SCORE = 1.0e4
N_FORCED = 3
WPAD = 16
ATT_SCALE = DH ** -0.5
MASK_BIAS = -(2.0 ** 30)
LOG2E = 1.4426950408889634
POOL_WINDOWS = (2, 4, 8, 16)
POOL_GROUP = D_MODEL // len(POOL_WINDOWS)
POOL_HALO = 16
HG_HEADS = 8
HG_DK = 128
HG_CHUNK = 64
HG_SUB = 8
HG_NH = 8

Z16_Q, Z16_KS, Z16_VS, Z16_KW, Z16_VW = 0, 1024, 1280, 1536, 1792
Z16_ASILU, Z16_BV, Z16_BSILU = 2048, 3072, 4096
Z16_CQ, Z16_CI, Z16_CSILU, Z16_MERGE = 5120, 6144, 7168, 8192
Z16_W = 11264
Z32_CF, Z32_KC, Z32_VC, Z32_GATE = 0, 1024, 1280, 1536
Z32_W = 2048
GATE_PAD = 128

VMEM_LIMIT = 56 * 1024 * 1024


def _cparams(sem):
    return pltpu.CompilerParams(dimension_semantics=sem, vmem_limit_bytes=VMEM_LIMIT)


def _sigmoid(x):
    return 1.0 / (1.0 + jnp.exp(-x))


def _silu(x):
    return x * _sigmoid(x)


def _inproj_kernel(x_ref, g_ref, w_ref, o_ref, h_ref):
    @pl.when(pl.program_id(1) == 0)
    def _():
        x = x_ref[...]
        ms = jnp.mean(x * x, axis=-1, keepdims=True)
        h_ref[...] = (x * lax.rsqrt(ms + NORM_EPS) * g_ref[...]).astype(BF16)

    o_ref[...] = jnp.dot(h_ref[...], w_ref[...], preferred_element_type=F32).astype(o_ref.dtype)


def _inproj(x2, gain, w, out_dtype, tm, tn):
    m, d = x2.shape
    n = w.shape[1]
    return pl.pallas_call(
        _inproj_kernel,
        grid=(m // tm, n // tn),
        in_specs=[pl.BlockSpec((tm, d), lambda i, j: (i, 0)),
                  pl.BlockSpec((1, d), lambda i, j: (0, 0)),
                  pl.BlockSpec((d, tn), lambda i, j: (0, j))],
        out_specs=pl.BlockSpec((tm, tn), lambda i, j: (i, j)),
        out_shape=jax.ShapeDtypeStruct((m, n), out_dtype),
        scratch_shapes=[pltpu.VMEM((tm, d), BF16)],
        compiler_params=_cparams(("parallel", "arbitrary")),
        name="inproj",
    )(x2, gain, w)


def _compress_kernel(x_ref, pe_ref, w_ref, o_ref, ot_ref, *, nch):
    acc_a = jnp.zeros((nch, CMP_LANES), F32)
    acc_b = jnp.zeros((nch, CMP_LANES), F32)
    for p in range(CMP_STRIDE):
        xp = x_ref[0, pl.ds(p, nch, stride=CMP_STRIDE), :]
        a = (xp + pe_ref[0, p:p + 1, :]).astype(BF16)
        b = (xp + pe_ref[0, CMP_STRIDE + p:CMP_STRIDE + p + 1, :]).astype(BF16)
        acc_a += jnp.dot(a, w_ref[0, p], preferred_element_type=F32)
        acc_b += jnp.dot(b, w_ref[0, CMP_STRIDE + p], preferred_element_type=F32)
    c = acc_a + pltpu.roll(acc_b, nch - 1, 0)
    ct = c.T
    ck = c * LOG2E
    for g in range(CMP_LANES // DH):
        o_ref[0, 0, g] = ck[:, g * DH:(g + 1) * DH].astype(BF16)
        ot_ref[0, 0, g] = ct[g * DH:(g + 1) * DH, :].astype(BF16)


def _compress(z32, pe, wbd, batch, seq):
    nch = seq // CMP_STRIDE
    z3 = z32.reshape(batch, seq, Z32_W)
    gpb = CMP_LANES // DH
    return pl.pallas_call(
        functools.partial(_compress_kernel, nch=nch),
        grid=(batch, 2, N_G // gpb),
        in_specs=[pl.BlockSpec((1, seq, CMP_LANES),
                               lambda b, s, c: (b, 0, (Z32_KC + s * KVW) // CMP_LANES + c)),
                  pl.BlockSpec((1, CMP_BLOCK, CMP_LANES), lambda b, s, c: (s, 0, 0)),
                  pl.BlockSpec((1, CMP_BLOCK, CMP_LANES, CMP_LANES), lambda b, s, c: (s, 0, 0, 0))],
        out_specs=[pl.BlockSpec((1, 1, gpb, nch, DH), lambda b, s, c: (b, s, c, 0, 0)),
                   pl.BlockSpec((1, 1, gpb, DH, nch), lambda b, s, c: (b, s, c, 0, 0))],
        out_shape=[jax.ShapeDtypeStruct((batch, 2, N_G, nch, DH), BF16),
                   jax.ShapeDtypeStruct((batch, 2, N_G, DH, nch), BF16)],
        compiler_params=_cparams(("parallel", "parallel", "parallel")),
        name="compress",
    )(z3, pe, wbd)


def _cmp_select_kernel(q_ref, ck_ref, cvt_ref, ov_ref, gate_ref, oc_ref, qa_ref, imp_ref, *, nch, n_s, n_sel):
    s0 = pl.program_id(2) * CQ
    qt = (q_ref[0].astype(F32).T * ATT_SCALE).astype(BF16)
    t_lane = s0 + lax.broadcasted_iota(jnp.int32, (1, CQ), 1)
    n_vis = (s0 + CQ - CMP_BLOCK) // CMP_STRIDE + 1

    def attend(rows):
        ck = ck_ref[0, 0, 0, 0:rows, :]
        cvt = cvt_ref[0, 0, 0, :, 0:rows]
        c_end = lax.broadcasted_iota(jnp.int32, (rows, 1), 0) * CMP_STRIDE + (CMP_BLOCK - 1)
        visible = c_end <= t_lane
        gates = _sigmoid(gate_ref[0])
        p_sum = jnp.zeros((rows, CQ), F32)
        outs = []
        for h in range(HPG):
            s = jnp.dot(ck, qt[h * DH:(h + 1) * DH, :], preferred_element_type=F32)
            s = jnp.where(visible, s, -jnp.inf)
            m = jnp.max(s, axis=0, keepdims=True)
            m = jnp.where(m == -jnp.inf, 0.0, m)
            e = jnp.exp2(s - m)
            p = e / jnp.maximum(jnp.sum(e, axis=0, keepdims=True), 1e-30)
            p_sum += p
            oct_h = jnp.dot(cvt, p.astype(BF16), preferred_element_type=F32)
            outs.append(oct_h.T * gates[:, 3 * h:3 * h + 1])
        oc_ref[0] = jnp.concatenate(outs, axis=1).astype(BF16)
        p_hi = p_sum.astype(BF16)
        p_lo = (p_sum - p_hi.astype(F32)).astype(BF16)
        ov = ov_ref[:, 0:rows]
        imp_ref[...] = (jnp.dot(ov, p_hi, preferred_element_type=F32)
                        + jnp.dot(ov, p_lo, preferred_element_type=F32))

    quantum = nch // CMP_ROW_LEVELS
    for j in range(CMP_ROW_LEVELS):
        rows = quantum * (j + 1)
        if j == 0:
            cond = n_vis <= rows
        else:
            cond = (n_vis > rows - quantum) & (n_vis <= rows)
        pl.when(cond)(functools.partial(attend, rows))

    blk = lax.broadcasted_iota(jnp.int32, (n_s, 1), 0)

    def pick(_, sc):
        m = jnp.max(sc, axis=0, keepdims=True)
        first = jnp.min(jnp.where(sc == m, blk, n_s), axis=0, keepdims=True)
        return jnp.where(blk == first, -jnp.inf, sc)

    biases = []
    for j in range(CQ // LANES):
        lanes = slice(j * LANES, (j + 1) * LANES)
        cur = t_lane[:, lanes] // SEL_BLOCK
        valid = blk <= cur
        force = (blk == 0) | (blk == cur) | (blk == cur - 1)
        score = jnp.where(valid, jnp.where(force, -jnp.inf, imp_ref[:, lanes]), -1.0)
        sc = lax.fori_loop(0, max(n_sel - N_FORCED, 0), pick, score, unroll=True)
        biases.append(jnp.where((sc == -jnp.inf) & valid, 0.0, MASK_BIAS).astype(BF16))
    bias_t = jnp.concatenate(biases, axis=1)
    flag_rows = jnp.where(lax.broadcasted_iota(jnp.int32, (WPAD, HPG * QT), 0) == 0, MASK_BIAS, 0.0).astype(BF16)
    for j in range(CQ // QT):
        lanes = slice(j * QT, (j + 1) * QT)
        qa_ref[0, 0, j, 0:n_s, :] = jnp.concatenate([bias_t[:, lanes]] * HPG, axis=1)
        qa_ref[0, 0, j, n_s:n_s + DH, :] = jnp.concatenate(
            [qt[h * DH:(h + 1) * DH, lanes] for h in range(HPG)], axis=1)
        qa_ref[0, 0, j, n_s + DH:n_s + DH + WPAD, :] = flag_rows


def _cmp_select(z16, z32, ck, cvt, ov, batch, seq):
    nch = seq // CMP_STRIDE
    n_s = seq // SEL_BLOCK
    n_sel = min(N_SELECT, n_s)
    z3 = z16.reshape(batch, seq, Z16_W)
    g3 = z32.reshape(batch, seq, Z32_W)
    return pl.pallas_call(
        functools.partial(_cmp_select_kernel, nch=nch, n_s=n_s, n_sel=n_sel),
        grid=(batch, N_G, seq // CQ),
        in_specs=[pl.BlockSpec((1, CQ, GW), lambda b, g, i: (b, i, Z16_Q // GW + g)),
                  pl.BlockSpec((1, 1, 1, nch, DH), lambda b, g, i: (b, 0, g, 0, 0)),
                  pl.BlockSpec((1, 1, 1, DH, nch), lambda b, g, i: (b, 1, g, 0, 0)),
                  pl.BlockSpec((n_s, nch), lambda b, g, i: (0, 0)),
                  pl.BlockSpec((1, CQ, GATE_PAD), lambda b, g, i: (b, i, Z32_GATE // GATE_PAD + g))],
        out_specs=[pl.BlockSpec((1, CQ, GW), lambda b, g, i: (b, i, g)),
                   pl.BlockSpec((1, 1, CQ // QT, n_s + DH + WPAD, HPG * QT), lambda b, g, i: (b, g, i, 0, 0))],
        out_shape=[jax.ShapeDtypeStruct((batch, seq, D_MODEL), BF16),
                   jax.ShapeDtypeStruct((batch, N_G, seq // QT, n_s + DH + WPAD, HPG * QT), BF16)],
        scratch_shapes=[pltpu.VMEM((n_s, CQ), F32)],
        compiler_params=_cparams(("parallel", "parallel", "parallel")),
        name="cmp_select",
    )(z3, ck, cvt, ov, g3)


def _kv_prep_kernel(ks_ref, vs_ref, kw_ref, vw_ref, ka_ref, va_ref, kwn_ref, vwa_ref, *, n_s, tt):
    t0 = pl.program_id(1) * tt
    ks = ks_ref[0]
    kw = kw_ref[0]
    vst = vs_ref[0].astype(F32).T
    vwt = vw_ref[0].astype(F32).T
    pos = t0 + lax.broadcasted_iota(jnp.int32, (tt, 1), 0)
    blk = lax.broadcasted_iota(jnp.int32, (1, n_s), 1)
    onehot = jnp.where(pos // SEL_BLOCK == blk, 1.0, 0.0).astype(BF16)
    ones_row = jnp.where(lax.broadcasted_iota(jnp.int32, (DH, tt), 0) == 0, 1.0, 0.0).astype(BF16)
    no_flag = jnp.zeros((tt, WPAD), BF16)
    for g in range(N_G):
        ka_ref[0, g, :, 0:n_s] = onehot
        ka_ref[0, g, :, n_s:n_s + DH] = ks[:, g * DH:(g + 1) * DH]
        ka_ref[0, g, :, n_s + DH:n_s + DH + WPAD] = no_flag
        kwn_ref[0, g, :, 0:DH] = kw[:, g * DH:(g + 1) * DH]
        kwn_ref[0, g, :, DH:DH + WPAD] = no_flag
        va_ref[0, g, 0:DH, :] = vst[g * DH:(g + 1) * DH, :].astype(BF16)
        va_ref[0, g, DH:2 * DH, :] = ones_row
        vwa_ref[0, g, 0:DH, :] = vwt[g * DH:(g + 1) * DH, :].astype(BF16)
        vwa_ref[0, g, DH:2 * DH, :] = ones_row


def _kv_prep(z16, batch, seq, tt):
    n_s = seq // SEL_BLOCK
    z3 = z16.reshape(batch, seq, Z16_W)

    def col(off):
        return pl.BlockSpec((1, tt, KVW), lambda b, i: (b, i, off // KVW))

    return pl.pallas_call(
        functools.partial(_kv_prep_kernel, n_s=n_s, tt=tt),
        grid=(batch, seq // tt),
        in_specs=[col(Z16_KS), col(Z16_VS), col(Z16_KW), col(Z16_VW)],
        out_specs=[pl.BlockSpec((1, N_G, tt, n_s + DH + WPAD), lambda b, i: (b, 0, i, 0)),
                   pl.BlockSpec((1, N_G, 2 * DH, tt), lambda b, i: (b, 0, 0, i)),
                   pl.BlockSpec((1, N_G, tt, DH + WPAD), lambda b, i: (b, 0, i, 0)),
                   pl.BlockSpec((1, N_G, 2 * DH, tt), lambda b, i: (b, 0, 0, i))],
        out_shape=[jax.ShapeDtypeStruct((batch, N_G, seq, n_s + DH + WPAD), BF16),
                   jax.ShapeDtypeStruct((batch, N_G, 2 * DH, seq), BF16),
                   jax.ShapeDtypeStruct((batch, N_G, seq, DH + WPAD), BF16),
                   jax.ShapeDtypeStruct((batch, N_G, 2 * DH, seq), BF16)],
        compiler_params=_cparams(("parallel", "parallel")),
        name="kv_prep",
    )(z3, z3, z3, z3)


def _nsa_attn_kernel(qa_ref, gate_ref, oc_ref, k_ref, v_ref, kw_ref, vw_ref, o_ref,
                     sa_ref, sb_ref, ma_ref, mb_ref, m_ref, acc_ref, accw_ref, *, tk, n_s, span):
    s0 = pl.program_id(2) * QT
    unit = 2 * tk
    n_units = s0 // unit
    u0 = pl.multiple_of(n_units * unit, unit)
    m_ref[...] = jnp.full(m_ref.shape, -jnp.inf, F32)
    acc_ref[...] = jnp.zeros(acc_ref.shape, F32)
    t_lane = s0 + lax.broadcasted_iota(jnp.int32, (1, HPG * QT), 1) % QT

    w0 = pl.multiple_of(s0, QT)
    sw = jnp.dot(kw_ref[0, 0, pl.ds(w0, span), :], qa_ref[0, 0, 0, n_s:n_s + DH + WPAD, :],
                 preferred_element_type=F32)
    q_loc = lax.broadcasted_iota(jnp.int32, (1, HPG * QT), 1) % QT
    r_loc = lax.broadcasted_iota(jnp.int32, (QT, 1), 0)
    slabs = [(0, jnp.where(r_loc > q_loc, sw[0:QT], -jnp.inf))]
    if WINDOW > QT:
        slabs.append((QT, sw[QT:WINDOW]))
    slabs.append((WINDOW, jnp.where(r_loc <= q_loc, sw[WINDOW:WINDOW + QT], -jnp.inf)))
    mw = functools.reduce(jnp.maximum, [jnp.max(s, axis=0, keepdims=True) for _, s in slabs])
    accw_ref[...] = functools.reduce(
        lambda x, y: x + y,
        [jnp.dot(vw_ref[0, 0, :, pl.ds(pl.multiple_of(w0 + r, QT), s.shape[0])], jnp.exp2(s - mw).astype(BF16),
                 preferred_element_type=F32) for r, s in slabs])

    def scores(off):
        return jnp.dot(k_ref[0, 0, pl.ds(off, tk), :], qa_ref[0, 0, 0], preferred_element_type=F32)

    def causal(s, off):
        kpos = off + lax.broadcasted_iota(jnp.int32, (tk, 1), 0)
        return jnp.where(kpos <= t_lane, s, MASK_BIAS)

    def stage(s_ref, mx_ref, off):
        s = scores(off)
        s_ref[...] = s
        mx_ref[...] = jnp.max(s, axis=0, keepdims=True)

    def accumulate(s, tile_max, off):
        m_prev = m_ref[...]
        m_new = jnp.maximum(m_prev, tile_max)
        p = jnp.exp2(s - m_new).astype(BF16)
        acc_ref[...] = jnp.exp2(m_prev - m_new) * acc_ref[...] + jnp.dot(
            v_ref[0, 0, :, pl.ds(off, tk)], p, preferred_element_type=F32)
        m_ref[...] = m_new

    def accumulate_causal(off):
        s = causal(scores(off), off)
        accumulate(s, jnp.max(s, axis=0, keepdims=True), off)

    stage(sa_ref, ma_ref, 0)

    accumulate_causal(u0)

    @pl.when(s0 >= u0 + tk)
    def _():
        accumulate_causal(pl.multiple_of(u0 + tk, tk))

    @pl.when(n_units > 0)
    def _():
        def body(u, carry):
            off = pl.multiple_of(u * unit, unit)
            stage(sb_ref, mb_ref, off + tk)
            accumulate(sa_ref[...], ma_ref[...], off)
            nxt = pl.multiple_of(jnp.minimum(u + 1, n_units - 1) * unit, unit)
            stage(sa_ref, ma_ref, nxt)
            accumulate(sb_ref[...], mb_ref[...], off + tk)
            return carry

        lax.fori_loop(0, n_units, body, 0)

    gates_t = _sigmoid(gate_ref[0]).T
    outs = []
    for h in range(HPG):
        a = acc_ref[:, h * QT:(h + 1) * QT]
        aw = accw_ref[:, h * QT:(h + 1) * QT]
        w_sel = gates_t[3 * h + 1:3 * h + 2, :] / a[DH:DH + 1, :]
        w_win = gates_t[3 * h + 2:3 * h + 3, :] / aw[DH:DH + 1, :]
        outs.append((a[0:DH, :] * w_sel + aw[0:DH, :] * w_win).T)
    o_ref[0] = (oc_ref[0].astype(F32) + jnp.concatenate(outs, axis=1)).astype(BF16)


def _nsa_attn(qaug, z32, oc, kaug, vaug, kwn, vwaug, batch, seq, tk):
    kd = kaug.shape[-1]
    span = WINDOW + QT
    g3 = z32.reshape(batch, seq, Z32_W)
    kwn = jnp.pad(kwn, ((0, 0), (0, 0), (WINDOW, 0), (0, 0)), constant_values=1.0)
    vwaug = jnp.pad(vwaug, ((0, 0), (0, 0), (0, 0), (WINDOW, 0)))

    def per_head(shape):
        return pl.BlockSpec((1, 1) + shape, lambda b, g, i: (b, g, 0, 0))

    return pl.pallas_call(
        functools.partial(_nsa_attn_kernel, tk=tk, n_s=kd - DH - WPAD, span=span),
        grid=(batch, N_G, seq // QT),
        in_specs=[pl.BlockSpec((1, 1, 1, kd, HPG * QT), lambda b, g, i: (b, g, i, 0, 0)),
                  pl.BlockSpec((1, QT, GATE_PAD), lambda b, g, i: (b, i, Z32_GATE // GATE_PAD + g)),
                  pl.BlockSpec((1, QT, GW), lambda b, g, i: (b, i, g)),
                  per_head((seq, kd)), per_head((2 * DH, seq)),
                  per_head((seq + WINDOW, DH + WPAD)), per_head((2 * DH, seq + WINDOW))],
        out_specs=pl.BlockSpec((1, QT, GW), lambda b, g, i: (b, i, g)),
        out_shape=jax.ShapeDtypeStruct((batch, seq, D_MODEL), BF16),
        scratch_shapes=[pltpu.VMEM((tk, HPG * QT), F32),
                        pltpu.VMEM((tk, HPG * QT), F32),
                        pltpu.VMEM((1, HPG * QT), F32),
                        pltpu.VMEM((1, HPG * QT), F32),
                        pltpu.VMEM((1, HPG * QT), F32),
                        pltpu.VMEM((2 * DH, HPG * QT), F32),
                        pltpu.VMEM((2 * DH, HPG * QT), F32)],
        compiler_params=_cparams(("parallel", "parallel", "arbitrary")),
        name="nsa_attn",
    )(qaug, g3, oc, kaug, vaug, kwn, vwaug)


def _pool_kernel(v_ref, prev_ref, gate_ref, w_ref, scale_ref, o_ref, *, tt):
    i = pl.program_id(1)
    cur = v_ref[0].astype(F32)
    prev = jnp.where(i > 0, prev_ref[0].astype(F32), 0.0)
    t = i * tt + lax.broadcasted_iota(jnp.int32, (tt, 1), 0)
    for g, w in enumerate(POOL_WINDOWS):
        sl = slice(g * POOL_GROUP, (g + 1) * POOL_GROUP)
        ext = jnp.concatenate([prev[:, sl], cur[:, sl]], axis=0)
        acc = ext
        d = 1
        while d < w:
            acc = acc + pltpu.roll(acc, d, 0)
            d *= 2
        cnt = jnp.minimum(t + 1, w).astype(F32)
        pooled = acc[POOL_HALO:, :] / cnt - cur[:, sl]
        mixed = jnp.dot(pooled.astype(BF16), w_ref[g], preferred_element_type=F32)
        o_ref[0, :, sl] = (mixed * scale_ref[:, sl] * _silu(gate_ref[0, :, sl].astype(F32))).astype(BF16)


def _pool(z16, w_pool, scale, batch, seq, tt):
    z3 = z16.reshape(batch, seq, Z16_W)
    hb = tt // POOL_HALO
    return pl.pallas_call(
        functools.partial(_pool_kernel, tt=tt),
        grid=(batch, seq // tt),
        in_specs=[pl.BlockSpec((1, tt, D_MODEL), lambda b, i: (b, i, Z16_BV // D_MODEL)),
                  pl.BlockSpec((1, POOL_HALO, D_MODEL),
                               lambda b, i: (b, jnp.maximum(i * hb - 1, 0), Z16_BV // D_MODEL)),
                  pl.BlockSpec((1, tt, D_MODEL), lambda b, i: (b, i, Z16_BSILU // D_MODEL)),
                  pl.BlockSpec((len(POOL_WINDOWS), POOL_GROUP, POOL_GROUP), lambda b, i: (0, 0, 0)),
                  pl.BlockSpec((1, D_MODEL), lambda b, i: (0, 0))],
        out_specs=pl.BlockSpec((1, tt, D_MODEL), lambda b, i: (b, i, 0)),
        out_shape=jax.ShapeDtypeStruct((batch, seq, D_MODEL), BF16),
        compiler_params=_cparams(("parallel", "parallel")),
        name="pool",
    )(z3, z3, z3, w_pool, scale)


def _hgrn_kernel(q_ref, f_ref, i_ref, gate_ref, lb_ref, gain_ref, o_ref, st_ref, *, n_chunks, nh):
    c, sub = HG_CHUNK, HG_SUB

    @pl.when(pl.program_id(2) == 0)
    def _():
        st_ref[...] = jnp.zeros(st_ref.shape, F32)

    row = lax.broadcasted_iota(jnp.int32, (c, 1), 0)
    tri = jnp.where(lax.broadcasted_iota(jnp.int32, (c, c), 1) <= row, 1.0, 0.0).astype(BF16)
    col_c = lax.broadcasted_iota(jnp.int32, (1, c), 1)
    nt = (((1,), (1,)), ((), ()))

    def head_chunk(r0, hd):
        cols = slice(hd * HG_DK, (hd + 1) * HG_DK)
        lb = lb_ref[0, :, cols]
        f = lb + (1.0 - lb) * _sigmoid(f_ref[0, pl.ds(r0, c), cols])
        lf = jnp.log2(f)
        k = 1.0 - f
        q = q_ref[0, pl.ds(r0, c), cols].astype(F32)
        v = i_ref[0, pl.ds(r0, c), cols]
        hi = lf.astype(BF16)
        r1 = lf - hi.astype(F32)
        mid = r1.astype(BF16)
        lo = (r1 - mid.astype(F32)).astype(BF16)
        a = (jnp.dot(tri, hi, preferred_element_type=F32)
             + jnp.dot(tri, mid, preferred_element_type=F32)
             + jnp.dot(tri, lo, preferred_element_type=F32))

        st = st_ref[hd]
        o = lax.dot_general((q * jnp.exp2(a)).astype(BF16), st.astype(BF16), nt, preferred_element_type=F32)

        att = jnp.zeros((c, c), F32)
        half = c // 2
        while half >= sub:
            pair = 2 * half
            ref = jnp.concatenate(
                [jnp.broadcast_to(a[p * pair + half - 1:p * pair + half, :], (pair, HG_DK))
                 for p in range(c // pair)], axis=0)
            right = (row % pair) >= half
            qt = jnp.where(right, q * jnp.exp2(jnp.minimum(a - ref, 0.0)), 0.0)
            kt = jnp.where(right, 0.0, k * jnp.exp2(jnp.minimum(ref - a, 0.0)))
            lvl = lax.dot_general(qt.astype(BF16), kt.astype(BF16), nt, preferred_element_type=F32)
            att = lvl if pair == c else att + jnp.where((row // pair) == (col_c // pair), lvl, 0.0)
            half //= 2
        q3 = q.reshape(c // sub, sub, HG_DK)
        k3 = k.reshape(c // sub, sub, HG_DK)
        f3 = f.reshape(c // sub, sub, HG_DK)
        dec = None
        diag = jnp.zeros((c, c), F32)
        for d in range(sub):
            if d == 0:
                prod = q3 * k3
            else:
                f_sh = f3 if d == 1 else pltpu.roll(f3, d - 1, 1)
                dec = f_sh if dec is None else dec * f_sh
                prod = q3 * pltpu.roll(k3, d, 1) * dec
            colv = jnp.sum(prod, axis=2, keepdims=True).reshape(c, 1)
            diag = jnp.where((col_c == row - d) & (row % sub >= d), colv, diag)
        o = o + jnp.dot((att + diag).astype(BF16), v, preferred_element_type=F32)

        a_last = a[c - 1:c, :]
        kd = (k * jnp.exp2(a_last - a)).astype(BF16)
        st_ref[hd] = st * jnp.exp2(a_last) + lax.dot_general(
            v, kd, (((0,), (0,)), ((), ())), preferred_element_type=F32)

        ms = jnp.mean(o * o, axis=-1, keepdims=True)
        y = o * lax.rsqrt(ms + NORM_EPS) * gain_ref[0, :, cols]
        o_ref[0, pl.ds(r0, c), cols] = (y * _silu(gate_ref[0, pl.ds(r0, c), cols].astype(F32))).astype(BF16)

    def chunk(n, carry):
        r0 = pl.multiple_of(n * c, c)
        for hd in range(nh):
            head_chunk(r0, hd)
        return carry

    lax.fori_loop(0, n_chunks, chunk, 0)


def _hgrn(z16, z32, lb, gain, batch, seq, tc, nh):
    z3 = z16.reshape(batch, seq, Z16_W)
    f3 = z32.reshape(batch, seq, Z32_W)
    w = nh * HG_DK

    def col16(off):
        return pl.BlockSpec((1, tc, w), lambda b, h, i: (b, i, off // w + h))

    return pl.pallas_call(
        functools.partial(_hgrn_kernel, n_chunks=tc // HG_CHUNK, nh=nh),
        grid=(batch, HG_HEADS // nh, seq // tc),
        in_specs=[col16(Z16_CQ),
                  pl.BlockSpec((1, tc, w), lambda b, h, i: (b, i, Z32_CF // w + h)),
                  col16(Z16_CI), col16(Z16_CSILU),
                  pl.BlockSpec((1, 1, w), lambda b, h, i: (h, 0, 0)),
                  pl.BlockSpec((1, 1, w), lambda b, h, i: (h, 0, 0))],
        out_specs=pl.BlockSpec((1, tc, w), lambda b, h, i: (b, i, h)),
        out_shape=jax.ShapeDtypeStruct((batch, seq, D_MODEL), BF16),
        scratch_shapes=[pltpu.VMEM((nh, HG_DK, HG_DK), F32)],
        compiler_params=_cparams(("parallel", "parallel", "arbitrary")),
        name="hgrn",
    )(z3, f3, z3, z3, lb, gain)


def _outproj_kernel(oa_ref, asilu_ref, ob_ref, ocn_ref, m0_ref, m1_ref, m2_ref,
                    x_ref, wb_ref, wo_ref, post_ref, o_ref):
    oa = (oa_ref[...].astype(F32) * _silu(asilu_ref[...].astype(F32))).astype(BF16)
    merged = _sigmoid(m0_ref[...].astype(F32)) * jnp.dot(oa, wb_ref[0], preferred_element_type=F32)
    merged += _sigmoid(m1_ref[...].astype(F32)) * jnp.dot(ob_ref[...], wb_ref[1], preferred_element_type=F32)
    merged += _sigmoid(m2_ref[...].astype(F32)) * jnp.dot(ocn_ref[...], wb_ref[2], preferred_element_type=F32)
    out = jnp.dot(merged.astype(BF16), wo_ref[...], preferred_element_type=F32)
    ms = jnp.mean(out * out, axis=-1, keepdims=True)
    o_ref[...] = x_ref[...] + out * lax.rsqrt(ms + NORM_EPS) * post_ref[...]


def _outproj(oa, z16, ob, ocn, x2, wb, wo, post, tm):
    m = x2.shape[0]

    def rows(j=0):
        return pl.BlockSpec((tm, D_MODEL), lambda i: (i, j))

    return pl.pallas_call(
        _outproj_kernel,
        grid=(m // tm,),
        in_specs=[rows(), rows(Z16_ASILU // D_MODEL), rows(), rows(),
                  rows(Z16_MERGE // D_MODEL), rows(Z16_MERGE // D_MODEL + 1), rows(Z16_MERGE // D_MODEL + 2),
                  rows(),
                  pl.BlockSpec((3, D_MODEL, D_MODEL), lambda i: (0, 0, 0)),
                  pl.BlockSpec((D_MODEL, D_MODEL), lambda i: (0, 0)),
                  pl.BlockSpec((1, D_MODEL), lambda i: (0, 0))],
        out_specs=rows(),
        out_shape=jax.ShapeDtypeStruct((m, D_MODEL), F32),
        compiler_params=_cparams(("parallel",)),
        name="outproj",
    )(oa, z16, ob, ocn, z16, z16, z16, x2, wb, wo, post)


def _split_w_in(w):
    sizes = (1024, 256, 256, 256, 256, 256, 256, 3 * N_G * HPG, 1024, 1024, 1024, 1024, 1024, 1024, 1024, 3072)
    names = ("a_q", "a_kc", "a_vc", "a_ks", "a_vs", "a_kw", "a_vw", "a_gate", "a_silu", "b_v", "b_silu",
             "c_q", "c_f", "c_i", "c_silu", "merge")
    parts, off = {}, 0
    for n, s in zip(names, sizes):
        parts[n] = w[:, off:off + s]
        off += s
    parts["a_ks"] = parts["a_ks"] * LOG2E
    parts["a_kw"] = parts["a_kw"] * LOG2E
    w16 = jnp.concatenate([parts[n] for n in ("a_q", "a_ks", "a_vs", "a_kw", "a_vw", "a_silu", "b_v", "b_silu",
                                              "c_q", "c_i", "c_silu", "merge")], axis=1)
    gate = parts["a_gate"].reshape(D_MODEL, N_G, 3 * HPG)
    gate = jnp.pad(gate, ((0, 0), (0, 0), (0, GATE_PAD - 3 * HPG))).reshape(D_MODEL, N_G * GATE_PAD)
    w32 = jnp.concatenate([parts["c_f"], parts["a_kc"], parts["a_vc"], gate], axis=1)
    return w16.astype(BF16), w32.astype(BF16)


def _block_diag_heads(w):
    n = CMP_LANES // DH
    eye = jnp.eye(n, dtype=w.dtype)
    return jnp.einsum("gh,pde->pgdhe", eye, w).reshape(CMP_BLOCK, CMP_LANES, CMP_LANES)


def _overlap_matrix(seq):
    n_c = seq // CMP_STRIDE
    n_s = seq // SEL_BLOCK
    c_start = jnp.arange(n_c) * CMP_STRIDE
    s_start = jnp.arange(n_s) * SEL_BLOCK
    ov = jnp.clip(jnp.minimum(c_start[None, :] + CMP_BLOCK, s_start[:, None] + SEL_BLOCK)
                  - jnp.maximum(c_start[None, :], s_start[:, None]), 0).astype(F32) / CMP_BLOCK
    return ov.astype(BF16)


def kernel(x, pre_norm, w_in, cmp_pe_k, cmp_pe_v, cmp_w_k, cmp_w_v, pool_w, pool_scale, hgrn_lb_logits, hgrn_norm,
           w_branch, w_out, post_norm):
    batch, seq, _ = x.shape
    depth = w_in.shape[0]
    m = batch * seq
    probs = jax.nn.softmax(hgrn_lb_logits.astype(F32), axis=0)
    lbs = jnp.cumsum(probs, axis=0) - probs[0]
    ov = _overlap_matrix(seq)
    tm_in = min(2048, m)
    tt = min(512, seq)

    x2 = x.reshape(m, D_MODEL)
    for l in range(depth):
        w16, w32 = _split_w_in(w_in[l])
        gain = pre_norm[l].reshape(1, D_MODEL)
        z16 = _inproj(x2, gain, w16, BF16, tm_in, 1024)
        z32 = _inproj(x2, gain, w32, F32, tm_in, 1024)

        pe = jnp.stack([jnp.tile(cmp_pe_k[l], (1, CMP_LANES // DH)), jnp.tile(cmp_pe_v[l], (1, CMP_LANES // DH))])
        wbd = jnp.stack([_block_diag_heads(cmp_w_k[l]), _block_diag_heads(cmp_w_v[l])]).astype(BF16)
        ckv, ckvt = _compress(z32, pe, wbd, batch, seq)
        oc, qaug = _cmp_select(z16, z32, ckv, ckvt, ov, batch, seq)
        kaug, vaug, kwn, vwaug = _kv_prep(z16, batch, seq, tt)
        oa = _nsa_attn(qaug, z32, oc, kaug, vaug, kwn, vwaug, batch, seq, min(SEL_TK, seq // 2))

        ob = _pool(z16, pool_w[l].astype(BF16), pool_scale[l].reshape(1, D_MODEL), batch, seq, tt)
        ocn = _hgrn(z16, z32, lbs[l].reshape(HG_HEADS // HG_NH, 1, HG_NH * HG_DK),
                    hgrn_norm[l].reshape(HG_HEADS // HG_NH, 1, HG_NH * HG_DK), batch, seq, tt, HG_NH)

        x2 = _outproj(oa.reshape(m, D_MODEL), z16, ob.reshape(m, D_MODEL), ocn.reshape(m, D_MODEL), x2,
                      w_branch[l].astype(BF16), w_out[l].astype(BF16), post_norm[l].reshape(1, D_MODEL), 256)
    return x2.reshape(batch, seq, D_MODEL)
```

```python
import functools

import jax
import jax.numpy as jnp
from jax import lax
from jax.experimental import pallas as pl
from jax.experimental.pallas import tpu as pltpu

F32 = jnp.float32
BF16 = jnp.bfloat16

D_MODEL = 1024
NORM_EPS = 1e-6
DH = 64
N_G = 4
HPG = 4
GW = HPG * DH
KVW = N_G * DH
CMP_BLOCK = 32
CMP_STRIDE = 16
CMP_LANES = 128
SEL_BLOCK = 64
N_SELECT = 16
WINDOW = 512
QT = 512
SEL_TK = 512
CQ = 512
CMP_ROW_LEVELS = 4
LANES = 128
FORCE_SCORE = 1.0e4
N_FORCED = 3
WPAD = 16
ATT_SCALE = DH ** -0.5
MASK_BIAS = -(2.0 ** 30)
LOG2E = 1.4426950408889634
POOL_WINDOWS = (2, 4, 8, 16)
POOL_GROUP = D_MODEL // len(POOL_WINDOWS)
POOL_HALO = 16
HG_HEADS = 8
HG_DK = 128
HG_CHUNK = 64
HG_SUB = 8
HG_NH = 8

Z16_Q, Z16_KS, Z16_VS, Z16_KW, Z16_VW = 0, 1024, 1280, 1536, 1792
Z16_ASILU, Z16_BV, Z16_BSILU = 2048, 3072, 4096
Z16_CQ, Z16_CI, Z16_CSILU, Z16_MERGE = 5120, 6144, 7168, 8192
Z16_W = 11264
Z32_CF, Z32_KC, Z32_VC, Z32_GATE = 0, 1024, 1280, 1536
Z32_W = 2048
GATE_PAD = 128

VMEM_LIMIT = 56 * 1024 * 1024


def _cparams(sem):
    return pltpu.CompilerParams(dimension_semantics=sem, vmem_limit_bytes=VMEM_LIMIT)


def _sigmoid(x):
    return 1.0 / (1.0 + jnp.exp(-x))


def _silu(x):
    return x * _sigmoid(x)


def _inproj_kernel(x_ref, g_ref, w_ref, o_ref, h_ref):
    @pl.when(pl.program_id(1) == 0)
    def _():
        x = x_ref[...]
        ms = jnp.mean(x * x, axis=-1, keepdims=True)
        h_ref[...] = (x * lax.rsqrt(ms + NORM_EPS) * g_ref[...]).astype(BF16)

    o_ref[...] = jnp.dot(h_ref[...], w_ref[...], preferred_element_type=F32).astype(o_ref.dtype)


def _inproj(x2, gain, w, out_dtype, tm, tn):
    m, d = x2.shape
    n = w.shape[1]
    return pl.pallas_call(
        _inproj_kernel,
        grid=(m // tm, n // tn),
        in_specs=[pl.BlockSpec((tm, d), lambda i, j: (i, 0)),
                  pl.BlockSpec((1, d), lambda i, j: (0, 0)),
                  pl.BlockSpec((d, tn), lambda i, j: (0, j))],
        out_specs=pl.BlockSpec((tm, tn), lambda i, j: (i, j)),
        out_shape=jax.ShapeDtypeStruct((m, n), out_dtype),
        scratch_shapes=[pltpu.VMEM((tm, d), BF16)],
        compiler_params=_cparams(("parallel", "arbitrary")),
        name="inproj",
    )(x2, gain, w)


def _compress_kernel(x_ref, pe_ref, w_ref, o_ref, ot_ref, *, nch):
    acc_a = jnp.zeros((nch, CMP_LANES), F32)
    acc_b = jnp.zeros((nch, CMP_LANES), F32)
    for p in range(CMP_STRIDE):
        xp = x_ref[0, pl.ds(p, nch, stride=CMP_STRIDE), :]
        a = (xp + pe_ref[0, p:p + 1, :]).astype(BF16)
        b = (xp + pe_ref[0, CMP_STRIDE + p:CMP_STRIDE + p + 1, :]).astype(BF16)
        acc_a += jnp.dot(a, w_ref[0, p], preferred_element_type=F32)
        acc_b += jnp.dot(b, w_ref[0, CMP_STRIDE + p], preferred_element_type=F32)
    c = acc_a + pltpu.roll(acc_b, nch - 1, 0)
    ct = c.T
    ck = c * LOG2E
    for g in range(CMP_LANES // DH):
        o_ref[0, 0, g] = ck[:, g * DH:(g + 1) * DH].astype(BF16)
        ot_ref[0, 0, g] = ct[g * DH:(g + 1) * DH, :].astype(BF16)


def _compress(z32, pe, wbd, batch, seq):
    nch = seq // CMP_STRIDE
    z3 = z32.reshape(batch, seq, Z32_W)
    gpb = CMP_LANES // DH
    return pl.pallas_call(
        functools.partial(_compress_kernel, nch=nch),
        grid=(batch, 2, N_G // gpb),
        in_specs=[pl.BlockSpec((1, seq, CMP_LANES),
                               lambda b, s, c: (b, 0, (Z32_KC + s * KVW) // CMP_LANES + c)),
                  pl.BlockSpec((1, CMP_BLOCK, CMP_LANES), lambda b, s, c: (s, 0, 0)),
                  pl.BlockSpec((1, CMP_BLOCK, CMP_LANES, CMP_LANES), lambda b, s, c: (s, 0, 0, 0))],
        out_specs=[pl.BlockSpec((1, 1, gpb, nch, DH), lambda b, s, c: (b, s, c, 0, 0)),
                   pl.BlockSpec((1, 1, gpb, DH, nch), lambda b, s, c: (b, s, c, 0, 0))],
        out_shape=[jax.ShapeDtypeStruct((batch, 2, N_G, nch, DH), BF16),
                   jax.ShapeDtypeStruct((batch, 2, N_G, DH, nch), BF16)],
        compiler_params=_cparams(("parallel", "parallel", "parallel")),
        name="compress",
    )(z3, pe, wbd)


def _cmp_select_kernel(q_ref, ck_ref, cvt_ref, ov_ref, gate_ref, oc_ref, qa_ref, imp_ref, *, nch, n_s, n_sel):
    s0 = pl.program_id(2) * CQ
    qt = (q_ref[0].astype(F32).T * ATT_SCALE).astype(BF16)
    t_lane = s0 + lax.broadcasted_iota(jnp.int32, (1, CQ), 1)
    n_vis = (s0 + CQ - CMP_BLOCK) // CMP_STRIDE + 1

    def attend(rows):
        ck = ck_ref[0, 0, 0, 0:rows, :]
        cvt = cvt_ref[0, 0, 0, :, 0:rows]
        c_end = lax.broadcasted_iota(jnp.int32, (rows, 1), 0) * CMP_STRIDE + (CMP_BLOCK - 1)
        visible = c_end <= t_lane
        gates = _sigmoid(gate_ref[0])
        p_sum = jnp.zeros((rows, CQ), F32)
        outs = []
        for h in range(HPG):
            s = jnp.dot(ck, qt[h * DH:(h + 1) * DH, :], preferred_element_type=F32)
            s = jnp.where(visible, s, -jnp.inf)
            m = jnp.max(s, axis=0, keepdims=True)
            m = jnp.where(m == -jnp.inf, 0.0, m)
            e = jnp.exp2(s - m)
            p = e / jnp.maximum(jnp.sum(e, axis=0, keepdims=True), 1e-30)
            p_sum += p
            oct_h = jnp.dot(cvt, p.astype(BF16), preferred_element_type=F32)
            outs.append(oct_h.T * gates[:, 3 * h:3 * h + 1])
        oc_ref[0] = jnp.concatenate(outs, axis=1).astype(BF16)
        p_hi = p_sum.astype(BF16)
        p_lo = (p_sum - p_hi.astype(F32)).astype(BF16)
        ov = ov_ref[:, 0:rows]
        imp_ref[...] = (jnp.dot(ov, p_hi, preferred_element_type=F32)
                        + jnp.dot(ov, p_lo, preferred_element_type=F32))

    quantum = nch // CMP_ROW_LEVELS
    for j in range(CMP_ROW_LEVELS):
        rows = quantum * (j + 1)
        if j == 0:
            cond = n_vis <= rows
        else:
            cond = (n_vis > rows - quantum) & (n_vis <= rows)
        pl.when(cond)(functools.partial(attend, rows))

    blk = lax.broadcasted_iota(jnp.int32, (n_s, 1), 0)

    def pick(_, sc):
        m = jnp.max(sc, axis=0, keepdims=True)
        first = jnp.min(jnp.where(sc == m, blk, n_s), axis=0, keepdims=True)
        return jnp.where(blk == first, -jnp.inf, sc)

    biases = []
    for j in range(CQ // LANES):
        lanes = slice(j * LANES, (j + 1) * LANES)
        cur = t_lane[:, lanes] // SEL_BLOCK
        valid = blk <= cur
        force = (blk == 0) | (blk == cur) | (blk == cur - 1)
        score = jnp.where(valid, jnp.where(force, -jnp.inf, imp_ref[:, lanes]), -1.0)
        sc = lax.fori_loop(0, max(n_sel - N_FORCED, 0), pick, score, unroll=True)
        biases.append(jnp.where((sc == -jnp.inf) & valid, 0.0, MASK_BIAS).astype(BF16))
    bias_t = jnp.concatenate(biases, axis=1)
    flag_rows = jnp.where(lax.broadcasted_iota(jnp.int32, (WPAD, HPG * QT), 0) == 0, MASK_BIAS, 0.0).astype(BF16)
    for j in range(CQ // QT):
        lanes = slice(j * QT, (j + 1) * QT)
        qa_ref[0, 0, j, 0:n_s, :] = jnp.concatenate([bias_t[:, lanes]] * HPG, axis=1)
        qa_ref[0, 0, j, n_s:n_s + DH, :] = jnp.concatenate(
            [qt[h * DH:(h + 1) * DH, lanes] for h in range(HPG)], axis=1)
        qa_ref[0, 0, j, n_s + DH:n_s + DH + WPAD, :] = flag_rows


def _cmp_select(z16, z32, ck, cvt, ov, batch, seq):
    nch = seq // CMP_STRIDE
    n_s = seq // SEL_BLOCK
    n_sel = min(N_SELECT, n_s)
    z3 = z16.reshape(batch, seq, Z16_W)
    g3 = z32.reshape(batch, seq, Z32_W)
    return pl.pallas_call(
        functools.partial(_cmp_select_kernel, nch=nch, n_s=n_s, n_sel=n_sel),
        grid=(batch, N_G, seq // CQ),
        in_specs=[pl.BlockSpec((1, CQ, GW), lambda b, g, i: (b, i, Z16_Q // GW + g)),
                  pl.BlockSpec((1, 1, 1, nch, DH), lambda b, g, i: (b, 0, g, 0, 0)),
                  pl.BlockSpec((1, 1, 1, DH, nch), lambda b, g, i: (b, 1, g, 0, 0)),
                  pl.BlockSpec((n_s, nch), lambda b, g, i: (0, 0)),
                  pl.BlockSpec((1, CQ, GATE_PAD), lambda b, g, i: (b, i, Z32_GATE // GATE_PAD + g))],
        out_specs=[pl.BlockSpec((1, CQ, GW), lambda b, g, i: (b, i, g)),
                   pl.BlockSpec((1, 1, CQ // QT, n_s + DH + WPAD, HPG * QT), lambda b, g, i: (b, g, i, 0, 0))],
        out_shape=[jax.ShapeDtypeStruct((batch, seq, D_MODEL), BF16),
                   jax.ShapeDtypeStruct((batch, N_G, seq // QT, n_s + DH + WPAD, HPG * QT), BF16)],
        scratch_shapes=[pltpu.VMEM((n_s, CQ), F32)],
        compiler_params=_cparams(("parallel", "parallel", "parallel")),
        name="cmp_select",
    )(z3, ck, cvt, ov, g3)


def _kv_prep_kernel(ks_ref, vs_ref, kw_ref, vw_ref, ka_ref, va_ref, kwn_ref, vwa_ref, *, n_s, tt):
    is_pad = pl.program_id(1) == 0
    t0 = jnp.maximum(pl.program_id(1) - 1, 0) * tt
    ks = ks_ref[0]
    kw = jnp.where(is_pad, 1.0, kw_ref[0].astype(F32)).astype(BF16)
    vst = vs_ref[0].astype(F32).T
    vwt = jnp.where(is_pad, 0.0, vw_ref[0].astype(F32).T)
    pos = t0 + lax.broadcasted_iota(jnp.int32, (tt, 1), 0)
    blk = lax.broadcasted_iota(jnp.int32, (1, n_s), 1)
    onehot = jnp.where(pos // SEL_BLOCK == blk, 1.0, 0.0).astype(BF16)
    ones_row = jnp.where(lax.broadcasted_iota(jnp.int32, (DH, tt), 0) == 0, 1.0, 0.0).astype(BF16)
    no_flag = jnp.zeros((tt, WPAD), BF16)
    w_flag = jnp.where(is_pad, 1.0, jnp.zeros((tt, WPAD), F32)).astype(BF16)
    for g in range(N_G):
        ka_ref[0, g, :, 0:n_s] = onehot
        ka_ref[0, g, :, n_s:n_s + DH] = ks[:, g * DH:(g + 1) * DH]
        ka_ref[0, g, :, n_s + DH:n_s + DH + WPAD] = no_flag
        kwn_ref[0, g, :, 0:DH] = kw[:, g * DH:(g + 1) * DH]
        kwn_ref[0, g, :, DH:DH + WPAD] = w_flag
        va_ref[0, g, 0:DH, :] = vst[g * DH:(g + 1) * DH, :].astype(BF16)
        va_ref[0, g, DH:2 * DH, :] = ones_row
        vwa_ref[0, g, 0:DH, :] = vwt[g * DH:(g + 1) * DH, :].astype(BF16)
        vwa_ref[0, g, DH:2 * DH, :] = ones_row


def _kv_prep(z16, batch, seq, tt):
    n_s = seq // SEL_BLOCK
    z3 = z16.reshape(batch, seq, Z16_W)

    assert tt == WINDOW, (tt, WINDOW)

    def tile(i):
        return jnp.maximum(i - 1, 0)

    def col(off):
        return pl.BlockSpec((1, tt, KVW), lambda b, i: (b, tile(i), off // KVW))

    return pl.pallas_call(
        functools.partial(_kv_prep_kernel, n_s=n_s, tt=tt),
        grid=(batch, seq // tt + 1),
        in_specs=[col(Z16_KS), col(Z16_VS), col(Z16_KW), col(Z16_VW)],
        out_specs=[pl.BlockSpec((1, N_G, tt, n_s + DH + WPAD), lambda b, i: (b, 0, tile(i), 0)),
                   pl.BlockSpec((1, N_G, 2 * DH, tt), lambda b, i: (b, 0, 0, tile(i))),
                   pl.BlockSpec((1, N_G, tt, DH + WPAD), lambda b, i: (b, 0, i, 0)),
                   pl.BlockSpec((1, N_G, 2 * DH, tt), lambda b, i: (b, 0, 0, i))],
        out_shape=[jax.ShapeDtypeStruct((batch, N_G, seq, n_s + DH + WPAD), BF16),
                   jax.ShapeDtypeStruct((batch, N_G, 2 * DH, seq), BF16),
                   jax.ShapeDtypeStruct((batch, N_G, seq + WINDOW, DH + WPAD), BF16),
                   jax.ShapeDtypeStruct((batch, N_G, 2 * DH, seq + WINDOW), BF16)],
        compiler_params=_cparams(("parallel", "arbitrary")),
        name="kv_prep",
    )(z3, z3, z3, z3)


def _nsa_attn_kernel(qa_ref, gate_ref, oc_ref, k_ref, v_ref, kw_ref, vw_ref, o_ref,
                     sa_ref, sb_ref, ma_ref, mb_ref, m_ref, acc_ref, accw_ref, *, tk, n_s, span):
    s0 = pl.program_id(2) * QT
    unit = 2 * tk
    n_units = s0 // unit
    u0 = pl.multiple_of(n_units * unit, unit)
    m_ref[...] = jnp.full(m_ref.shape, -jnp.inf, F32)
    acc_ref[...] = jnp.zeros(acc_ref.shape, F32)

    w0 = pl.multiple_of(s0, QT)
    sw = jnp.dot(kw_ref[0, 0, pl.ds(w0, span), :], qa_ref[0, 0, 0, n_s:n_s + DH + WPAD, :],
                 preferred_element_type=F32)
    q_loc = lax.broadcasted_iota(jnp.int32, (1, HPG * QT), 1) % QT
    r_loc = lax.broadcasted_iota(jnp.int32, (QT, 1), 0)
    slabs = [(0, jnp.where(r_loc > q_loc, sw[0:QT], -jnp.inf))]
    if WINDOW > QT:
        slabs.append((QT, sw[QT:WINDOW]))
    slabs.append((WINDOW, jnp.where(r_loc <= q_loc, sw[WINDOW:WINDOW + QT], -jnp.inf)))
    mw = functools.reduce(jnp.maximum, [jnp.max(s, axis=0, keepdims=True) for _, s in slabs])
    accw_ref[...] = functools.reduce(
        lambda x, y: x + y,
        [jnp.dot(vw_ref[0, 0, :, pl.ds(pl.multiple_of(w0 + r, QT), s.shape[0])], jnp.exp2(s - mw).astype(BF16),
                 preferred_element_type=F32) for r, s in slabs])

    def scores(off):
        return jnp.dot(k_ref[0, 0, pl.ds(off, tk), :], qa_ref[0, 0, 0], preferred_element_type=F32)

    def stage(s_ref, mx_ref, off):
        s = scores(off)
        s_ref[...] = s
        mx_ref[...] = jnp.max(s, axis=0, keepdims=True)

    def accumulate(s, tile_max, off):
        m_prev = m_ref[...]
        m_new = jnp.maximum(m_prev, tile_max)
        p = jnp.exp2(s - m_new).astype(BF16)
        acc_ref[...] = jnp.exp2(m_prev - m_new) * acc_ref[...] + jnp.dot(
            v_ref[0, 0, :, pl.ds(off, tk)], p, preferred_element_type=F32)
        m_ref[...] = m_new

    stage(sa_ref, ma_ref, 0)

    s_own = jnp.where(r_loc <= q_loc, scores(pl.multiple_of(s0, tk)), MASK_BIAS)
    accumulate(s_own, jnp.max(s_own, axis=0, keepdims=True), pl.multiple_of(s0, tk))

    @pl.when(s0 >= u0 + tk)
    def _():
        s_odd = scores(u0)
        accumulate(s_odd, jnp.max(s_odd, axis=0, keepdims=True), u0)

    @pl.when(n_units > 0)
    def _():
        def body(u, carry):
            off = pl.multiple_of(u * unit, unit)
            stage(sb_ref, mb_ref, off + tk)
            accumulate(sa_ref[...], ma_ref[...], off)
            nxt = pl.multiple_of(jnp.minimum(u + 1, n_units - 1) * unit, unit)
            stage(sa_ref, ma_ref, nxt)
            accumulate(sb_ref[...], mb_ref[...], off + tk)
            return carry

        lax.fori_loop(0, n_units, body, 0)

    gates_t = _sigmoid(gate_ref[0]).T
    outs = []
    for h in range(HPG):
        a = acc_ref[:, h * QT:(h + 1) * QT]
        aw = accw_ref[:, h * QT:(h + 1) * QT]
        w_sel = gates_t[3 * h + 1:3 * h + 2, :] / a[DH:DH + 1, :]
        w_win = gates_t[3 * h + 2:3 * h + 3, :] / aw[DH:DH + 1, :]
        outs.append((a[0:DH, :] * w_sel + aw[0:DH, :] * w_win).T)
    o_ref[0] = (oc_ref[0].astype(F32) + jnp.concatenate(outs, axis=1)).astype(BF16)


def _nsa_attn(qaug, z32, oc, kaug, vaug, kwn, vwaug, batch, seq, tk):
    assert tk == QT and seq % (2 * tk) == 0, (tk, QT, seq)
    kd = kaug.shape[-1]
    span = WINDOW + QT
    g3 = z32.reshape(batch, seq, Z32_W)

    def per_head(shape):
        return pl.BlockSpec((1, 1) + shape, lambda b, g, i: (b, g, 0, 0))

    return pl.pallas_call(
        functools.partial(_nsa_attn_kernel, tk=tk, n_s=kd - DH - WPAD, span=span),
        grid=(batch, N_G, seq // QT),
        in_specs=[pl.BlockSpec((1, 1, 1, kd, HPG * QT), lambda b, g, i: (b, g, i, 0, 0)),
                  pl.BlockSpec((1, QT, GATE_PAD), lambda b, g, i: (b, i, Z32_GATE // GATE_PAD + g)),
                  pl.BlockSpec((1, QT, GW), lambda b, g, i: (b, i, g)),
                  per_head((seq, kd)), per_head((2 * DH, seq)),
                  per_head((seq + WINDOW, DH + WPAD)), per_head((2 * DH, seq + WINDOW))],
        out_specs=pl.BlockSpec((1, QT, GW), lambda b, g, i: (b, i, g)),
        out_shape=jax.ShapeDtypeStruct((batch, seq, D_MODEL), BF16),
        scratch_shapes=[pltpu.VMEM((tk, HPG * QT), F32),
                        pltpu.VMEM((tk, HPG * QT), F32),
                        pltpu.VMEM((1, HPG * QT), F32),
                        pltpu.VMEM((1, HPG * QT), F32),
                        pltpu.VMEM((1, HPG * QT), F32),
                        pltpu.VMEM((2 * DH, HPG * QT), F32),
                        pltpu.VMEM((2 * DH, HPG * QT), F32)],
        compiler_params=_cparams(("parallel", "parallel", "arbitrary")),
        name="nsa_attn",
    )(qaug, g3, oc, kaug, vaug, kwn, vwaug)


def _pool_kernel(v_ref, prev_ref, gate_ref, w_ref, scale_ref, o_ref, *, tt):
    i = pl.program_id(1)
    cur = v_ref[0].astype(F32)
    prev = jnp.where(i > 0, prev_ref[0].astype(F32), 0.0)
    t = i * tt + lax.broadcasted_iota(jnp.int32, (tt, 1), 0)
    for g, w in enumerate(POOL_WINDOWS):
        sl = slice(g * POOL_GROUP, (g + 1) * POOL_GROUP)
        ext = jnp.concatenate([prev[:, sl], cur[:, sl]], axis=0)
        acc = ext
        d = 1
        while d < w:
            acc = acc + pltpu.roll(acc, d, 0)
            d *= 2
        cnt = jnp.minimum(t + 1, w).astype(F32)
        pooled = acc[POOL_HALO:, :] / cnt - cur[:, sl]
        mixed = jnp.dot(pooled.astype(BF16), w_ref[g], preferred_element_type=F32)
        o_ref[0, :, sl] = (mixed * scale_ref[:, sl] * _silu(gate_ref[0, :, sl].astype(F32))).astype(BF16)


def _pool(z16, w_pool, scale, batch, seq, tt):
    z3 = z16.reshape(batch, seq, Z16_W)
    hb = tt // POOL_HALO
    return pl.pallas_call(
        functools.partial(_pool_kernel, tt=tt),
        grid=(batch, seq // tt),
        in_specs=[pl.BlockSpec((1, tt, D_MODEL), lambda b, i: (b, i, Z16_BV // D_MODEL)),
                  pl.BlockSpec((1, POOL_HALO, D_MODEL),
                               lambda b, i: (b, jnp.maximum(i * hb - 1, 0), Z16_BV // D_MODEL)),
                  pl.BlockSpec((1, tt, D_MODEL), lambda b, i: (b, i, Z16_BSILU // D_MODEL)),
                  pl.BlockSpec((len(POOL_WINDOWS), POOL_GROUP, POOL_GROUP), lambda b, i: (0, 0, 0)),
                  pl.BlockSpec((1, D_MODEL), lambda b, i: (0, 0))],
        out_specs=pl.BlockSpec((1, tt, D_MODEL), lambda b, i: (b, i, 0)),
        out_shape=jax.ShapeDtypeStruct((batch, seq, D_MODEL), BF16),
        compiler_params=_cparams(("parallel", "parallel")),
        name="pool",
    )(z3, z3, z3, w_pool, scale)


def _hgrn_kernel(q_ref, f_ref, i_ref, gate_ref, lb_ref, gain_ref, o_ref, st_ref, *, n_chunks, nh):
    c, sub = HG_CHUNK, HG_SUB

    @pl.when(pl.program_id(2) == 0)
    def _():
        st_ref[...] = jnp.zeros(st_ref.shape, F32)

    row = lax.broadcasted_iota(jnp.int32, (c, 1), 0)
    tri = jnp.where(lax.broadcasted_iota(jnp.int32, (c, c), 1) <= row, 1.0, 0.0).astype(BF16)
    col_c = lax.broadcasted_iota(jnp.int32, (1, c), 1)
    nt = (((1,), (1,)), ((), ()))

    def head_chunk(r0, hd):
        cols = slice(hd * HG_DK, (hd + 1) * HG_DK)
        lb = lb_ref[0, :, cols]
        f = lb + (1.0 - lb) * _sigmoid(f_ref[0, pl.ds(r0, c), cols])
        lf = jnp.log2(f)
        k = 1.0 - f
        q = q_ref[0, pl.ds(r0, c), cols].astype(F32)
        v = i_ref[0, pl.ds(r0, c), cols]
        hi = lf.astype(BF16)
        r1 = lf - hi.astype(F32)
        mid = r1.astype(BF16)
        lo = (r1 - mid.astype(F32)).astype(BF16)
        a = (jnp.dot(tri, hi, preferred_element_type=F32)
             + jnp.dot(tri, mid, preferred_element_type=F32)
             + jnp.dot(tri, lo, preferred_element_type=F32))

        st = st_ref[hd]
        o = lax.dot_general((q * jnp.exp2(a)).astype(BF16), st.astype(BF16), nt, preferred_element_type=F32)

        att = jnp.zeros((c, c), F32)
        half = c // 2
        while half >= sub:
            pair = 2 * half
            ref = jnp.concatenate(
                [jnp.broadcast_to(a[p * pair + half - 1:p * pair + half, :], (pair, HG_DK))
                 for p in range(c // pair)], axis=0)
            right = (row % pair) >= half
            qt = jnp.where(right, q * jnp.exp2(jnp.minimum(a - ref, 0.0)), 0.0)
            kt = jnp.where(right, 0.0, k * jnp.exp2(jnp.minimum(ref - a, 0.0)))
            lvl = lax.dot_general(qt.astype(BF16), kt.astype(BF16), nt, preferred_element_type=F32)
            att = lvl if pair == c else att + jnp.where((row // pair) == (col_c // pair), lvl, 0.0)
            half //= 2
        q3 = q.reshape(c // sub, sub, HG_DK)
        k3 = k.reshape(c // sub, sub, HG_DK)
        f3 = f.reshape(c // sub, sub, HG_DK)
        dec = None
        diag = jnp.zeros((c, c), F32)
        for d in range(sub):
            if d == 0:
                prod = q3 * k3
            else:
                f_sh = f3 if d == 1 else pltpu.roll(f3, d - 1, 1)
                dec = f_sh if dec is None else dec * f_sh
                prod = q3 * pltpu.roll(k3, d, 1) * dec
            colv = jnp.sum(prod, axis=2, keepdims=True).reshape(c, 1)
            diag = jnp.where((col_c == row - d) & (row % sub >= d), colv, diag)
        o = o + jnp.dot((att + diag).astype(BF16), v, preferred_element_type=F32)

        a_last = a[c - 1:c, :]
        kd = (k * jnp.exp2(a_last - a)).astype(BF16)
        st_ref[hd] = st * jnp.exp2(a_last) + lax.dot_general(
            v, kd, (((0,), (0,)), ((), ())), preferred_element_type=F32)

        ms = jnp.mean(o * o, axis=-1, keepdims=True)
        y = o * lax.rsqrt(ms + NORM_EPS) * gain_ref[0, :, cols]
        o_ref[0, pl.ds(r0, c), cols] = (y * _silu(gate_ref[0, pl.ds(r0, c), cols].astype(F32))).astype(BF16)

    def chunk(n, carry):
        r0 = pl.multiple_of(n * c, c)
        for hd in range(nh):
            head_chunk(r0, hd)
        return carry

    lax.fori_loop(0, n_chunks, chunk, 0, unroll=2)


def _hgrn(z16, z32, lb, gain, batch, seq, tc, nh):
    z3 = z16.reshape(batch, seq, Z16_W)
    f3 = z32.reshape(batch, seq, Z32_W)
    w = nh * HG_DK

    def col16(off):
        return pl.BlockSpec((1, tc, w), lambda b, h, i: (b, i, off // w + h))

    return pl.pallas_call(
        functools.partial(_hgrn_kernel, n_chunks=tc // HG_CHUNK, nh=nh),
        grid=(batch, HG_HEADS // nh, seq // tc),
        in_specs=[col16(Z16_CQ),
                  pl.BlockSpec((1, tc, w), lambda b, h, i: (b, i, Z32_CF // w + h)),
                  col16(Z16_CI), col16(Z16_CSILU),
                  pl.BlockSpec((1, 1, w), lambda b, h, i: (h, 0, 0)),
                  pl.BlockSpec((1, 1, w), lambda b, h, i: (h, 0, 0))],
        out_specs=pl.BlockSpec((1, tc, w), lambda b, h, i: (b, i, h)),
        out_shape=jax.ShapeDtypeStruct((batch, seq, D_MODEL), BF16),
        scratch_shapes=[pltpu.VMEM((nh, HG_DK, HG_DK), F32)],
        compiler_params=_cparams(("parallel", "parallel", "arbitrary")),
        name="hgrn",
    )(z3, f3, z3, z3, lb, gain)


def _outproj_kernel(oa_ref, asilu_ref, ob_ref, ocn_ref, m0_ref, m1_ref, m2_ref,
                    x_ref, wb_ref, wo_ref, post_ref, o_ref):
    oa = (oa_ref[...].astype(F32) * _silu(asilu_ref[...].astype(F32))).astype(BF16)
    merged = _sigmoid(m0_ref[...].astype(F32)) * jnp.dot(oa, wb_ref[0], preferred_element_type=F32)
    merged += _sigmoid(m1_ref[...].astype(F32)) * jnp.dot(ob_ref[...], wb_ref[1], preferred_element_type=F32)
    merged += _sigmoid(m2_ref[...].astype(F32)) * jnp.dot(ocn_ref[...], wb_ref[2], preferred_element_type=F32)
    out = jnp.dot(merged.astype(BF16), wo_ref[...], preferred_element_type=F32)
    ms = jnp.mean(out * out, axis=-1, keepdims=True)
    o_ref[...] = x_ref[...] + out * lax.rsqrt(ms + NORM_EPS) * post_ref[...]


def _outproj(oa, z16, ob, ocn, x2, wb, wo, post, tm):
    m = x2.shape[0]

    def rows(j=0):
        return pl.BlockSpec((tm, D_MODEL), lambda i: (i, j))

    return pl.pallas_call(
        _outproj_kernel,
        grid=(m // tm,),
        in_specs=[rows(), rows(Z16_ASILU // D_MODEL), rows(), rows(),
                  rows(Z16_MERGE // D_MODEL), rows(Z16_MERGE // D_MODEL + 1), rows(Z16_MERGE // D_MODEL + 2),
                  rows(),
                  pl.BlockSpec((3, D_MODEL, D_MODEL), lambda i: (0, 0, 0)),
                  pl.BlockSpec((D_MODEL, D_MODEL), lambda i: (0, 0)),
                  pl.BlockSpec((1, D_MODEL), lambda i: (0, 0))],
        out_specs=rows(),
        out_shape=jax.ShapeDtypeStruct((m, D_MODEL), F32),
        compiler_params=_cparams(("parallel",)),
        name="outproj",
    )(oa, z16, ob, ocn, z16, z16, z16, x2, wb, wo, post)


def _split_w_in(w):
    sizes = (1024, 256, 256, 256, 256, 256, 256, 3 * N_G * HPG, 1024, 1024, 1024, 1024, 1024, 1024, 1024, 3072)
    names = ("a_q", "a_kc", "a_vc", "a_ks", "a_vs", "a_kw", "a_vw", "a_gate", "a_silu", "b_v", "b_silu",
             "c_q", "c_f", "c_i", "c_silu", "merge")
    parts, off = {}, 0
    for n, s in zip(names, sizes):
        parts[n] = w[:, off:off + s]
        off += s
    parts["a_ks"] = parts["a_ks"] * LOG2E
    parts["a_kw"] = parts["a_kw"] * LOG2E
    w16 = jnp.concatenate([parts[n] for n in ("a_q", "a_ks", "a_vs", "a_kw", "a_vw", "a_silu", "b_v", "b_silu",
                                              "c_q", "c_i", "c_silu", "merge")], axis=1)
    gate = parts["a_gate"].reshape(D_MODEL, N_G, 3 * HPG)
    gate = jnp.pad(gate, ((0, 0), (0, 0), (0, GATE_PAD - 3 * HPG))).reshape(D_MODEL, N_G * GATE_PAD)
    w32 = jnp.concatenate([parts["c_f"], parts["a_kc"], parts["a_vc"], gate], axis=1)
    return w16.astype(BF16), w32.astype(BF16)


def _block_diag_heads(w):
    n = CMP_LANES // DH
    eye = jnp.eye(n, dtype=w.dtype)
    return jnp.einsum("gh,pde->pgdhe", eye, w).reshape(CMP_BLOCK, CMP_LANES, CMP_LANES)


def _overlap_matrix(seq):
    n_c = seq // CMP_STRIDE
    n_s = seq // SEL_BLOCK
    c_start = jnp.arange(n_c) * CMP_STRIDE
    s_start = jnp.arange(n_s) * SEL_BLOCK
    ov = jnp.clip(jnp.minimum(c_start[None, :] + CMP_BLOCK, s_start[:, None] + SEL_BLOCK)
                  - jnp.maximum(c_start[None, :], s_start[:, None]), 0).astype(F32) / CMP_BLOCK
    return ov.astype(BF16)


def kernel(x, pre_norm, w_in, cmp_pe_k, cmp_pe_v, cmp_w_k, cmp_w_v, pool_w, pool_scale, hgrn_lb_logits, hgrn_norm,
           w_branch, w_out, post_norm):
    batch, seq, _ = x.shape
    depth = w_in.shape[0]
    m = batch * seq
    probs = jax.nn.softmax(hgrn_lb_logits.astype(F32), axis=0)
    lbs = jnp.cumsum(probs, axis=0) - probs[0]
    ov = _overlap_matrix(seq)
    tm_in = min(2048, m)
    tt = min(512, seq)

    x2 = x.reshape(m, D_MODEL)
    for l in range(depth):
        w16, w32 = _split_w_in(w_in[l])
        gain = pre_norm[l].reshape(1, D_MODEL)
        z16 = _inproj(x2, gain, w16, BF16, tm_in, 1024)
        z32 = _inproj(x2, gain, w32, F32, tm_in, 1024)

        pe = jnp.stack([jnp.tile(cmp_pe_k[l], (1, CMP_LANES // DH)), jnp.tile(cmp_pe_v[l], (1, CMP_LANES // DH))])
        wbd = jnp.stack([_block_diag_heads(cmp_w_k[l]), _block_diag_heads(cmp_w_v[l])]).astype(BF16)
        ckv, ckvt = _compress(z32, pe, wbd, batch, seq)
        oc, qaug = _cmp_select(z16, z32, ckv, ckvt, ov, batch, seq)
        kaug, vaug, kwn, vwaug = _kv_prep(z16, batch, seq, tt)
        oa = _nsa_attn(qaug, z32, oc, kaug, vaug, kwn, vwaug, batch, seq, min(SEL_TK, seq // 2))

        ob = _pool(z16, pool_w[l].astype(BF16), pool_scale[l].reshape(1, D_MODEL), batch, seq, tt)
        ocn = _hgrn(z16, z32, lbs[l].reshape(HG_HEADS // HG_NH, 1, HG_NH * HG_DK),
                    hgrn_norm[l].reshape(HG_HEADS // HG_NH, 1, HG_NH * HG_DK), batch, seq, tt, HG_NH)

        x2 = _outproj(oa.reshape(m, D_MODEL), z16, ob.reshape(m, D_MODEL), ocn.reshape(m, D_MODEL), x2,
                      w_branch[l].astype(BF16), w_out[l].astype(BF16), post_norm[l].reshape(1, D_MODEL), 512)
    return x2.reshape(batch, seq, D_MODEL)
```

```python
import functools

import jax
import jax.numpy as jnp
from jax import lax
from jax.experimental import pallas as pl
from jax.experimental.pallas import tpu as pltpu

F32 = jnp.float32
BF16 = jnp.bfloat16

D_MODEL = 1024
NORM_EPS = 1e-6
DH = 64
N_G = 4
HPG = 4
GW = HPG * DH
KVW = N_G * DH
CMP_BLOCK = 32
CMP_STRIDE = 16
CMP_LANES = 128
SEL_BLOCK = 64
N_SELECT = 16
WINDOW = 512
QT = 512
SEL_TK = 512
CQ = 512
CMP_ROW_LEVELS = 4
LANES = 128
FORCE_SCORE = 1.0e4
N_FORCED = 3
WPAD = 16
ATT_SCALE = DH ** -0.5
MASK_BIAS = -(2.0 ** 30)
LOG2E = 1.4426950408889634
POOL_WINDOWS = (2, 4, 8, 16)
POOL_GROUP = D_MODEL // len(POOL_WINDOWS)
POOL_HALO = 16
HG_HEADS = 8
HG_DK = 128
HG_CHUNK = 64
HG_SUB = 8
HG_NH = 8

Z16_Q, Z16_KS, Z16_VS, Z16_KW, Z16_VW = 0, 1024, 1280, 1536, 1792
Z16_ASILU, Z16_BV, Z16_BSILU = 2048, 3072, 4096
Z16_CQ, Z16_CI, Z16_CSILU, Z16_MERGE = 5120, 6144, 7168, 8192
Z16_W = 11264
Z32_CF, Z32_KC, Z32_VC, Z32_GATE = 0, 1024, 1280, 1536
Z32_W = 2048
GATE_PAD = 128

VMEM_LIMIT = 56 * 1024 * 1024


def _cparams(sem):
    return pltpu.CompilerParams(dimension_semantics=sem, vmem_limit_bytes=VMEM_LIMIT)


def _sigmoid(x):
    return 1.0 / (1.0 + jnp.exp(-x))


def _silu(x):
    return x * _sigmoid(x)


def _inproj_kernel(x_ref, g_ref, w_ref, o_ref, h_ref):
    @pl.when(pl.program_id(1) == 0)
    def _():
        x = x_ref[...]
        ms = jnp.mean(x * x, axis=-1, keepdims=True)
        h_ref[...] = (x * lax.rsqrt(ms + NORM_EPS) * g_ref[...]).astype(BF16)

    o_ref[...] = jnp.dot(h_ref[...], w_ref[...], preferred_element_type=F32).astype(o_ref.dtype)


def _inproj(x2, gain, w, out_dtype, tm, tn):
    m, d = x2.shape
    n = w.shape[1]
    return pl.pallas_call(
        _inproj_kernel,
        grid=(m // tm, n // tn),
        in_specs=[pl.BlockSpec((tm, d), lambda i, j: (i, 0)),
                  pl.BlockSpec((1, d), lambda i, j: (0, 0)),
                  pl.BlockSpec((d, tn), lambda i, j: (0, j))],
        out_specs=pl.BlockSpec((tm, tn), lambda i, j: (i, j)),
        out_shape=jax.ShapeDtypeStruct((m, n), out_dtype),
        scratch_shapes=[pltpu.VMEM((tm, d), BF16)],
        compiler_params=_cparams(("parallel", "arbitrary")),
        name="inproj",
    )(x2, gain, w)


def _compress_kernel(x_ref, pe_ref, w_ref, o_ref, ot_ref, *, nch):
    acc_a = jnp.zeros((nch, CMP_LANES), F32)
    acc_b = jnp.zeros((nch, CMP_LANES), F32)
    for p in range(CMP_STRIDE):
        xp = x_ref[0, pl.ds(p, nch, stride=CMP_STRIDE), :]
        a = (xp + pe_ref[0, p:p + 1, :]).astype(BF16)
        b = (xp + pe_ref[0, CMP_STRIDE + p:CMP_STRIDE + p + 1, :]).astype(BF16)
        acc_a += jnp.dot(a, w_ref[0, p], preferred_element_type=F32)
        acc_b += jnp.dot(b, w_ref[0, CMP_STRIDE + p], preferred_element_type=F32)
    c = acc_a + pltpu.roll(acc_b, nch - 1, 0)
    ct = c.T
    ck = c * LOG2E
    for g in range(CMP_LANES // DH):
        o_ref[0, 0, g] = ck[:, g * DH:(g + 1) * DH].astype(BF16)
        ot_ref[0, 0, g] = ct[g * DH:(g + 1) * DH, :].astype(BF16)


def _compress(z32, pe, wbd, batch, seq):
    nch = seq // CMP_STRIDE
    z3 = z32.reshape(batch, seq, Z32_W)
    gpb = CMP_LANES // DH
    return pl.pallas_call(
        functools.partial(_compress_kernel, nch=nch),
        grid=(batch, 2, N_G // gpb),
        in_specs=[pl.BlockSpec((1, seq, CMP_LANES),
                               lambda b, s, c: (b, 0, (Z32_KC + s * KVW) // CMP_LANES + c)),
                  pl.BlockSpec((1, CMP_BLOCK, CMP_LANES), lambda b, s, c: (s, 0, 0)),
                  pl.BlockSpec((1, CMP_BLOCK, CMP_LANES, CMP_LANES), lambda b, s, c: (s, 0, 0, 0))],
        out_specs=[pl.BlockSpec((1, 1, gpb, nch, DH), lambda b, s, c: (b, s, c, 0, 0)),
                   pl.BlockSpec((1, 1, gpb, DH, nch), lambda b, s, c: (b, s, c, 0, 0))],
        out_shape=[jax.ShapeDtypeStruct((batch, 2, N_G, nch, DH), BF16),
                   jax.ShapeDtypeStruct((batch, 2, N_G, DH, nch), BF16)],
        compiler_params=_cparams(("parallel", "parallel", "parallel")),
        name="compress",
    )(z3, pe, wbd)


def _cmp_select_kernel(q_ref, ck_ref, cvt_ref, ov_ref, gate_ref, oc_ref, qa_ref, imp_ref, *, nch, n_s, n_sel):
    s0 = pl.program_id(2) * CQ
    qt = (q_ref[0].astype(F32).T * ATT_SCALE).astype(BF16)
    t_lane = s0 + lax.broadcasted_iota(jnp.int32, (1, CQ), 1)
    n_vis = (s0 + CQ - CMP_BLOCK) // CMP_STRIDE + 1

    def attend(rows):
        ck = ck_ref[0, 0, 0, 0:rows, :]
        cvt = cvt_ref[0, 0, 0, :, 0:rows]
        c_end = lax.broadcasted_iota(jnp.int32, (rows, 1), 0) * CMP_STRIDE + (CMP_BLOCK - 1)
        visible = c_end <= t_lane
        gates = _sigmoid(gate_ref[0])
        p_sum = jnp.zeros((rows, CQ), F32)
        outs = []
        for h in range(HPG):
            s = jnp.dot(ck, qt[h * DH:(h + 1) * DH, :], preferred_element_type=F32)
            s = jnp.where(visible, s, -jnp.inf)
            m = jnp.max(s, axis=0, keepdims=True)
            m = jnp.where(m == -jnp.inf, 0.0, m)
            e = jnp.exp2(s - m)
            p = e / jnp.maximum(jnp.sum(e, axis=0, keepdims=True), 1e-30)
            p_sum += p
            oct_h = jnp.dot(cvt, p.astype(BF16), preferred_element_type=F32)
            outs.append(oct_h.T * gates[:, 3 * h:3 * h + 1])
        oc_ref[0] = jnp.concatenate(outs, axis=1).astype(BF16)
        p_hi = p_sum.astype(BF16)
        p_lo = (p_sum - p_hi.astype(F32)).astype(BF16)
        ov = ov_ref[:, 0:rows]
        imp_ref[...] = (jnp.dot(ov, p_hi, preferred_element_type=F32)
                        + jnp.dot(ov, p_lo, preferred_element_type=F32))

    quantum = nch // CMP_ROW_LEVELS
    for j in range(CMP_ROW_LEVELS):
        rows = quantum * (j + 1)
        if j == 0:
            cond = n_vis <= rows
        else:
            cond = (n_vis > rows - quantum) & (n_vis <= rows)
        pl.when(cond)(functools.partial(attend, rows))

    blk = lax.broadcasted_iota(jnp.int32, (n_s, 1), 0)

    def pick(_, sc):
        m = jnp.max(sc, axis=0, keepdims=True)
        first = jnp.min(jnp.where(sc == m, blk, n_s), axis=0, keepdims=True)
        return jnp.where(blk == first, -jnp.inf, sc)

    biases = []
    for j in range(CQ // LANES):
        lanes = slice(j * LANES, (j + 1) * LANES)
        cur = t_lane[:, lanes] // SEL_BLOCK
        valid = blk <= cur
        force = (blk == 0) | (blk == cur) | (blk == cur - 1)
        score = jnp.where(valid, jnp.where(force, -jnp.inf, imp_ref[:, lanes]), -1.0)
        sc = lax.fori_loop(0, max(n_sel - N_FORCED, 0), pick, score, unroll=True)
        biases.append(jnp.where((sc == -jnp.inf) & valid, 0.0, MASK_BIAS).astype(BF16))
    bias_t = jnp.concatenate(biases, axis=1)
    flag_rows = jnp.where(lax.broadcasted_iota(jnp.int32, (WPAD, HPG * QT), 0) == 0, MASK_BIAS, 0.0).astype(BF16)
    for j in range(CQ // QT):
        lanes = slice(j * QT, (j + 1) * QT)
        qa_ref[0, 0, j, 0:n_s, :] = jnp.concatenate([bias_t[:, lanes]] * HPG, axis=1)
        qa_ref[0, 0, j, n_s:n_s + DH, :] = jnp.concatenate(
            [qt[h * DH:(h + 1) * DH, lanes] for h in range(HPG)], axis=1)
        qa_ref[0, 0, j, n_s + DH:n_s + DH + WPAD, :] = flag_rows


def _cmp_select(z16, z32, ck, cvt, ov, batch, seq):
    nch = seq // CMP_STRIDE
    n_s = seq // SEL_BLOCK
    n_sel = min(N_SELECT, n_s)
    z3 = z16.reshape(batch, seq, Z16_W)
    g3 = z32.reshape(batch, seq, Z32_W)
    return pl.pallas_call(
        functools.partial(_cmp_select_kernel, nch=nch, n_s=n_s, n_sel=n_sel),
        grid=(batch, N_G, seq // CQ),
        in_specs=[pl.BlockSpec((1, CQ, GW), lambda b, g, i: (b, i, Z16_Q // GW + g)),
                  pl.BlockSpec((1, 1, 1, nch, DH), lambda b, g, i: (b, 0, g, 0, 0)),
                  pl.BlockSpec((1, 1, 1, DH, nch), lambda b, g, i: (b, 1, g, 0, 0)),
                  pl.BlockSpec((n_s, nch), lambda b, g, i: (0, 0)),
                  pl.BlockSpec((1, CQ, GATE_PAD), lambda b, g, i: (b, i, Z32_GATE // GATE_PAD + g))],
        out_specs=[pl.BlockSpec((1, CQ, GW), lambda b, g, i: (b, i, g)),
                   pl.BlockSpec((1, 1, CQ // QT, n_s + DH + WPAD, HPG * QT), lambda b, g, i: (b, g, i, 0, 0))],
        out_shape=[jax.ShapeDtypeStruct((batch, seq, D_MODEL), BF16),
                   jax.ShapeDtypeStruct((batch, N_G, seq // QT, n_s + DH + WPAD, HPG * QT), BF16)],
        scratch_shapes=[pltpu.VMEM((n_s, CQ), F32)],
        compiler_params=_cparams(("parallel", "parallel", "parallel")),
        name="cmp_select",
    )(z3, ck, cvt, ov, g3)


def _kv_prep_kernel(ks_ref, vs_ref, kw_ref, vw_ref, ka_ref, va_ref, kwn_ref, vwa_ref, *, n_s, tt):
    is_pad = pl.program_id(1) == 0
    t0 = jnp.maximum(pl.program_id(1) - 1, 0) * tt
    ks = ks_ref[0]
    kw = jnp.where(is_pad, 1.0, kw_ref[0].astype(F32)).astype(BF16)
    vst = vs_ref[0].astype(F32).T
    vwt = jnp.where(is_pad, 0.0, vw_ref[0].astype(F32).T)
    pos = t0 + lax.broadcasted_iota(jnp.int32, (tt, 1), 0)
    blk = lax.broadcasted_iota(jnp.int32, (1, n_s), 1)
    onehot = jnp.where(pos // SEL_BLOCK == blk, 1.0, 0.0).astype(BF16)
    ones_row = jnp.where(lax.broadcasted_iota(jnp.int32, (DH, tt), 0) == 0, 1.0, 0.0).astype(BF16)
    no_flag = jnp.zeros((tt, WPAD), BF16)
    w_flag = jnp.where(is_pad, 1.0, jnp.zeros((tt, WPAD), F32)).astype(BF16)
    for g in range(N_G):
        ka_ref[0, g, :, 0:n_s] = onehot
        ka_ref[0, g, :, n_s:n_s + DH] = ks[:, g * DH:(g + 1) * DH]
        ka_ref[0, g, :, n_s + DH:n_s + DH + WPAD] = no_flag
        kwn_ref[0, g, :, 0:DH] = kw[:, g * DH:(g + 1) * DH]
        kwn_ref[0, g, :, DH:DH + WPAD] = w_flag
        va_ref[0, g, 0:DH, :] = vst[g * DH:(g + 1) * DH, :].astype(BF16)
        va_ref[0, g, DH:2 * DH, :] = ones_row
        vwa_ref[0, g, 0:DH, :] = vwt[g * DH:(g + 1) * DH, :].astype(BF16)
        vwa_ref[0, g, DH:2 * DH, :] = ones_row


def _kv_prep(z16, batch, seq, tt):
    n_s = seq // SEL_BLOCK
    z3 = z16.reshape(batch, seq, Z16_W)

    assert tt == WINDOW, (tt, WINDOW)

    def tile(i):
        return jnp.maximum(i - 1, 0)

    def col(off):
        return pl.BlockSpec((1, tt, KVW), lambda b, i: (b, tile(i), off // KVW))

    return pl.pallas_call(
        functools.partial(_kv_prep_kernel, n_s=n_s, tt=tt),
        grid=(batch, seq // tt + 1),
        in_specs=[col(Z16_KS), col(Z16_VS), col(Z16_KW), col(Z16_VW)],
        out_specs=[pl.BlockSpec((1, N_G, tt, n_s + DH + WPAD), lambda b, i: (b, 0, tile(i), 0)),
                   pl.BlockSpec((1, N_G, 2 * DH, tt), lambda b, i: (b, 0, 0, tile(i))),
                   pl.BlockSpec((1, N_G, tt, DH + WPAD), lambda b, i: (b, 0, i, 0)),
                   pl.BlockSpec((1, N_G, 2 * DH, tt), lambda b, i: (b, 0, 0, i))],
        out_shape=[jax.ShapeDtypeStruct((batch, N_G, seq, n_s + DH + WPAD), BF16),
                   jax.ShapeDtypeStruct((batch, N_G, 2 * DH, seq), BF16),
                   jax.ShapeDtypeStruct((batch, N_G, seq + WINDOW, DH + WPAD), BF16),
                   jax.ShapeDtypeStruct((batch, N_G, 2 * DH, seq + WINDOW), BF16)],
        compiler_params=_cparams(("parallel", "arbitrary")),
        name="kv_prep",
    )(z3, z3, z3, z3)


def _nsa_attn_kernel(qa_ref, gate_ref, oc_ref, k_ref, v_ref, kw_ref, vw_ref, o_ref,
                     sa_ref, sb_ref, ma_ref, mb_ref, m_ref, acc_ref, accw_ref, *, tk, n_s, span):
    s0 = pl.program_id(2) * QT
    unit = 2 * tk
    n_units = s0 // unit
    u0 = pl.multiple_of(n_units * unit, unit)
    m_ref[...] = jnp.full(m_ref.shape, -jnp.inf, F32)
    acc_ref[...] = jnp.zeros(acc_ref.shape, F32)

    w0 = pl.multiple_of(s0, QT)
    sw = jnp.dot(kw_ref[0, 0, pl.ds(w0, span), :], qa_ref[0, 0, 0, n_s:n_s + DH + WPAD, :],
                 preferred_element_type=F32)
    q_loc = lax.broadcasted_iota(jnp.int32, (1, HPG * QT), 1) % QT
    r_loc = lax.broadcasted_iota(jnp.int32, (QT, 1), 0)
    slabs = [(0, jnp.where(r_loc > q_loc, sw[0:QT], -jnp.inf))]
    if WINDOW > QT:
        slabs.append((QT, sw[QT:WINDOW]))
    slabs.append((WINDOW, jnp.where(r_loc <= q_loc, sw[WINDOW:WINDOW + QT], -jnp.inf)))
    mw = functools.reduce(jnp.maximum, [jnp.max(s, axis=0, keepdims=True) for _, s in slabs])
    accw_ref[...] = functools.reduce(
        lambda x, y: x + y,
        [jnp.dot(vw_ref[0, 0, :, pl.ds(pl.multiple_of(w0 + r, QT), s.shape[0])], jnp.exp2(s - mw).astype(BF16),
                 preferred_element_type=F32) for r, s in slabs])

    def scores(off):
        return jnp.dot(k_ref[0, 0, pl.ds(off, tk), :], qa_ref[0, 0, 0], preferred_element_type=F32)

    def stage(s_ref, mx_ref, off):
        s = scores(off)
        s_ref[...] = s
        mx_ref[...] = jnp.max(s, axis=0, keepdims=True)

    def accumulate(s, tile_max, off):
        m_prev = m_ref[...]
        m_new = jnp.maximum(m_prev, tile_max)
        p = jnp.exp2(s - m_new).astype(BF16)
        acc_ref[...] = jnp.exp2(m_prev - m_new) * acc_ref[...] + jnp.dot(
            v_ref[0, 0, :, pl.ds(off, tk)], p, preferred_element_type=F32)
        m_ref[...] = m_new

    stage(sa_ref, ma_ref, 0)

    s_own = jnp.where(r_loc <= q_loc, scores(pl.multiple_of(s0, tk)), MASK_BIAS)
    accumulate(s_own, jnp.max(s_own, axis=0, keepdims=True), pl.multiple_of(s0, tk))

    @pl.when(s0 >= u0 + tk)
    def _():
        s_odd = scores(u0)
        accumulate(s_odd, jnp.max(s_odd, axis=0, keepdims=True), u0)

    @pl.when(n_units > 0)
    def _():
        def body(u, carry):
            off = pl.multiple_of(u * unit, unit)
            stage(sb_ref, mb_ref, off + tk)
            accumulate(sa_ref[...], ma_ref[...], off)
            nxt = pl.multiple_of(jnp.minimum(u + 1, n_units - 1) * unit, unit)
            stage(sa_ref, ma_ref, nxt)
            accumulate(sb_ref[...], mb_ref[...], off + tk)
            return carry

        lax.fori_loop(0, n_units, body, 0)

    gates_t = _sigmoid(gate_ref[0]).T
    outs = []
    for h in range(HPG):
        a = acc_ref[:, h * QT:(h + 1) * QT]
        aw = accw_ref[:, h * QT:(h + 1) * QT]
        w_sel = gates_t[3 * h + 1:3 * h + 2, :] / a[DH:DH + 1, :]
        w_win = gates_t[3 * h + 2:3 * h + 3, :] / aw[DH:DH + 1, :]
        outs.append((a[0:DH, :] * w_sel + aw[0:DH, :] * w_win).T)
    o_ref[0] = (oc_ref[0].astype(F32) + jnp.concatenate(outs, axis=1)).astype(BF16)


def _nsa_attn(qaug, z32, oc, kaug, vaug, kwn, vwaug, batch, seq, tk):
    assert tk == QT and seq % (2 * tk) == 0, (tk, QT, seq)
    kd = kaug.shape[-1]
    span = WINDOW + QT
    g3 = z32.reshape(batch, seq, Z32_W)

    def per_head(shape):
        return pl.BlockSpec((1, 1) + shape, lambda b, g, i: (b, g, 0, 0))

    return pl.pallas_call(
        functools.partial(_nsa_attn_kernel, tk=tk, n_s=kd - DH - WPAD, span=span),
        grid=(batch, N_G, seq // QT),
        in_specs=[pl.BlockSpec((1, 1, 1, kd, HPG * QT), lambda b, g, i: (b, g, i, 0, 0)),
                  pl.BlockSpec((1, QT, GATE_PAD), lambda b, g, i: (b, i, Z32_GATE // GATE_PAD + g)),
                  pl.BlockSpec((1, QT, GW), lambda b, g, i: (b, i, g)),
                  per_head((seq, kd)), per_head((2 * DH, seq)),
                  per_head((seq + WINDOW, DH + WPAD)), per_head((2 * DH, seq + WINDOW))],
        out_specs=pl.BlockSpec((1, QT, GW), lambda b, g, i: (b, i, g)),
        out_shape=jax.ShapeDtypeStruct((batch, seq, D_MODEL), BF16),
        scratch_shapes=[pltpu.VMEM((tk, HPG * QT), F32),
                        pltpu.VMEM((tk, HPG * QT), F32),
                        pltpu.VMEM((1, HPG * QT), F32),
                        pltpu.VMEM((1, HPG * QT), F32),
                        pltpu.VMEM((1, HPG * QT), F32),
                        pltpu.VMEM((2 * DH, HPG * QT), F32),
                        pltpu.VMEM((2 * DH, HPG * QT), F32)],
        compiler_params=_cparams(("parallel", "parallel", "arbitrary")),
        name="nsa_attn",
    )(qaug, g3, oc, kaug, vaug, kwn, vwaug)


def _pool_kernel(v_ref, prev_ref, gate_ref, w_ref, scale_ref, o_ref, *, tt):
    i = pl.program_id(1)
    cur = v_ref[0].astype(F32)
    prev = jnp.where(i > 0, prev_ref[0].astype(F32), 0.0)
    t = i * tt + lax.broadcasted_iota(jnp.int32, (tt, 1), 0)
    for g, w in enumerate(POOL_WINDOWS):
        sl = slice(g * POOL_GROUP, (g + 1) * POOL_GROUP)
        ext = jnp.concatenate([prev[:, sl], cur[:, sl]], axis=0)
        acc = ext
        d = 1
        while d < w:
            acc = acc + pltpu.roll(acc, d, 0)
            d *= 2
        cnt = jnp.minimum(t + 1, w).astype(F32)
        pooled = acc[POOL_HALO:, :] / cnt - cur[:, sl]
        mixed = jnp.dot(pooled.astype(BF16), w_ref[g], preferred_element_type=F32)
        o_ref[0, :, sl] = (mixed * scale_ref[:, sl] * _silu(gate_ref[0, :, sl].astype(F32))).astype(BF16)


def _pool(z16, w_pool, scale, batch, seq, tt):
    z3 = z16.reshape(batch, seq, Z16_W)
    hb = tt // POOL_HALO
    return pl.pallas_call(
        functools.partial(_pool_kernel, tt=tt),
        grid=(batch, seq // tt),
        in_specs=[pl.BlockSpec((1, tt, D_MODEL), lambda b, i: (b, i, Z16_BV // D_MODEL)),
                  pl.BlockSpec((1, POOL_HALO, D_MODEL),
                               lambda b, i: (b, jnp.maximum(i * hb - 1, 0), Z16_BV // D_MODEL)),
                  pl.BlockSpec((1, tt, D_MODEL), lambda b, i: (b, i, Z16_BSILU // D_MODEL)),
                  pl.BlockSpec((len(POOL_WINDOWS), POOL_GROUP, POOL_GROUP), lambda b, i: (0, 0, 0)),
                  pl.BlockSpec((1, D_MODEL), lambda b, i: (0, 0))],
        out_specs=pl.BlockSpec((1, tt, D_MODEL), lambda b, i: (b, i, 0)),
        out_shape=jax.ShapeDtypeStruct((batch, seq, D_MODEL), BF16),
        compiler_params=_cparams(("parallel", "parallel")),
        name="pool",
    )(z3, z3, z3, w_pool, scale)


def _hgrn_kernel(q_ref, f_ref, i_ref, gate_ref, lb_ref, gain_ref, o_ref, st_ref, *, n_chunks, nh):
    c, sub = HG_CHUNK, HG_SUB

    @pl.when(pl.program_id(2) == 0)
    def _():
        st_ref[...] = jnp.zeros(st_ref.shape, F32)

    row = lax.broadcasted_iota(jnp.int32, (c, 1), 0)
    tri = jnp.where(lax.broadcasted_iota(jnp.int32, (c, c), 1) <= row, 1.0, 0.0).astype(BF16)
    col_c = lax.broadcasted_iota(jnp.int32, (1, c), 1)
    nt = (((1,), (1,)), ((), ()))

    def head_chunk(r0, hd):
        cols = slice(hd * HG_DK, (hd + 1) * HG_DK)
        lb = lb_ref[0, :, cols]
        f = lb + (1.0 - lb) * _sigmoid(f_ref[0, pl.ds(r0, c), cols])
        lf = jnp.log2(f)
        k = 1.0 - f
        q = q_ref[0, pl.ds(r0, c), cols].astype(F32)
        v = i_ref[0, pl.ds(r0, c), cols]
        hi = lf.astype(BF16)
        r1 = lf - hi.astype(F32)
        mid = r1.astype(BF16)
        lo = (r1 - mid.astype(F32)).astype(BF16)
        a = (jnp.dot(tri, hi, preferred_element_type=F32)
             + jnp.dot(tri, mid, preferred_element_type=F32)
             + jnp.dot(tri, lo, preferred_element_type=F32))

        st = st_ref[hd]
        o = lax.dot_general((q * jnp.exp2(a)).astype(BF16), st.astype(BF16), nt, preferred_element_type=F32)

        att = jnp.zeros((c, c), F32)
        half = c // 2
        while half >= sub:
            pair = 2 * half
            ref = jnp.concatenate(
                [jnp.broadcast_to(a[p * pair + half - 1:p * pair + half, :], (pair, HG_DK))
                 for p in range(c // pair)], axis=0)
            right = (row % pair) >= half
            qt = jnp.where(right, q * jnp.exp2(jnp.minimum(a - ref, 0.0)), 0.0)
            kt = jnp.where(right, 0.0, k * jnp.exp2(jnp.minimum(ref - a, 0.0)))
            lvl = lax.dot_general(qt.astype(BF16), kt.astype(BF16), nt, preferred_element_type=F32)
            att = lvl if pair == c else att + jnp.where((row // pair) == (col_c // pair), lvl, 0.0)
            half //= 2
        q3 = q.reshape(c // sub, sub, HG_DK)
        k3 = k.reshape(c // sub, sub, HG_DK)
        f3 = f.reshape(c // sub, sub, HG_DK)
        dec = None
        diag = jnp.zeros((c, c), F32)
        for d in range(sub):
            if d == 0:
                prod = q3 * k3
            else:
                f_sh = f3 if d == 1 else pltpu.roll(f3, d - 1, 1)
                dec = f_sh if dec is None else dec * f_sh
                prod = q3 * pltpu.roll(k3, d, 1) * dec
            colv = jnp.sum(prod, axis=2, keepdims=True).reshape(c, 1)
            diag = jnp.where((col_c == row - d) & (row % sub >= d), colv, diag)
        o = o + jnp.dot((att + diag).astype(BF16), v, preferred_element_type=F32)

        a_last = a[c - 1:c, :]
        kd = (k * jnp.exp2(a_last - a)).astype(BF16)
        st_ref[hd] = st * jnp.exp2(a_last) + lax.dot_general(
            v, kd, (((0,), (0,)), ((), ())), preferred_element_type=F32)

        ms = jnp.mean(o * o, axis=-1, keepdims=True)
        y = o * lax.rsqrt(ms + NORM_EPS) * gain_ref[0, :, cols]
        o_ref[0, pl.ds(r0, c), cols] = (y * _silu(gate_ref[0, pl.ds(r0, c), cols].astype(F32))).astype(BF16)

    def chunk(n, carry):
        r0 = pl.multiple_of(n * c, c)
        for hd in range(nh):
            head_chunk(r0, hd)
        return carry

    lax.fori_loop(0, n_chunks, chunk, 0, unroll=4)


def _hgrn(z16, z32, lb, gain, batch, seq, tc, nh):
    z3 = z16.reshape(batch, seq, Z16_W)
    f3 = z32.reshape(batch, seq, Z32_W)
    w = nh * HG_DK

    def col16(off):
        return pl.BlockSpec((1, tc, w), lambda b, h, i: (b, i, off // w + h))

    return pl.pallas_call(
        functools.partial(_hgrn_kernel, n_chunks=tc // HG_CHUNK, nh=nh),
        grid=(batch, HG_HEADS // nh, seq // tc),
        in_specs=[col16(Z16_CQ),
                  pl.BlockSpec((1, tc, w), lambda b, h, i: (b, i, Z32_CF // w + h)),
                  col16(Z16_CI), col16(Z16_CSILU),
                  pl.BlockSpec((1, 1, w), lambda b, h, i: (h, 0, 0)),
                  pl.BlockSpec((1, 1, w), lambda b, h, i: (h, 0, 0))],
        out_specs=pl.BlockSpec((1, tc, w), lambda b, h, i: (b, i, h)),
        out_shape=jax.ShapeDtypeStruct((batch, seq, D_MODEL), BF16),
        scratch_shapes=[pltpu.VMEM((nh, HG_DK, HG_DK), F32)],
        compiler_params=_cparams(("parallel", "parallel", "arbitrary")),
        name="hgrn",
    )(z3, f3, z3, z3, lb, gain)


def _outproj_kernel(oa_ref, asilu_ref, ob_ref, ocn_ref, m0_ref, m1_ref, m2_ref,
                    x_ref, wb_ref, wo_ref, post_ref, o_ref):
    oa = (oa_ref[...].astype(F32) * _silu(asilu_ref[...].astype(F32))).astype(BF16)
    merged = _sigmoid(m0_ref[...].astype(F32)) * jnp.dot(oa, wb_ref[0], preferred_element_type=F32)
    merged += _sigmoid(m1_ref[...].astype(F32)) * jnp.dot(ob_ref[...], wb_ref[1], preferred_element_type=F32)
    merged += _sigmoid(m2_ref[...].astype(F32)) * jnp.dot(ocn_ref[...], wb_ref[2], preferred_element_type=F32)
    out = jnp.dot(merged.astype(BF16), wo_ref[...], preferred_element_type=F32)
    ms = jnp.mean(out * out, axis=-1, keepdims=True)
    o_ref[...] = x_ref[...] + out * lax.rsqrt(ms + NORM_EPS) * post_ref[...]


def _outproj(oa, z16, ob, ocn, x2, wb, wo, post, tm):
    m = x2.shape[0]

    def rows(j=0):
        return pl.BlockSpec((tm, D_MODEL), lambda i: (i, j))

    return pl.pallas_call(
        _outproj_kernel,
        grid=(m // tm,),
        in_specs=[rows(), rows(Z16_ASILU // D_MODEL), rows(), rows(),
                  rows(Z16_MERGE // D_MODEL), rows(Z16_MERGE // D_MODEL + 1), rows(Z16_MERGE // D_MODEL + 2),
                  rows(),
                  pl.BlockSpec((3, D_MODEL, D_MODEL), lambda i: (0, 0, 0)),
                  pl.BlockSpec((D_MODEL, D_MODEL), lambda i: (0, 0)),
                  pl.BlockSpec((1, D_MODEL), lambda i: (0, 0))],
        out_specs=rows(),
        out_shape=jax.ShapeDtypeStruct((m, D_MODEL), F32),
        compiler_params=_cparams(("parallel",)),
        name="outproj",
    )(oa, z16, ob, ocn, z16, z16, z16, x2, wb, wo, post)


W_IN_SIZES = (1024, 256, 256, 256, 256, 256, 256, 3 * N_G * HPG, 1024, 1024, 1024, 1024, 1024, 1024, 1024, 3072)
W_IN_NAMES = ("a_q", "a_kc", "a_vc", "a_ks", "a_vs", "a_kw", "a_vw", "a_gate", "a_silu", "b_v", "b_silu",
              "c_q", "c_f", "c_i", "c_silu", "merge")


def _w_in_to_bf16(w_in):
    scale, off = [], 0
    for n, s in zip(W_IN_NAMES, W_IN_SIZES):
        scale += [LOG2E if n in ("a_ks", "a_kw") else 1.0] * s
        off += s
    return (w_in * jnp.asarray(scale, F32)).astype(BF16)


def _split_w_in(w):
    parts, off = {}, 0
    for n, s in zip(W_IN_NAMES, W_IN_SIZES):
        parts[n] = w[:, off:off + s]
        off += s
    w16 = jnp.concatenate([parts[n] for n in ("a_q", "a_ks", "a_vs", "a_kw", "a_vw", "a_silu", "b_v", "b_silu",
                                              "c_q", "c_i", "c_silu", "merge")], axis=1)
    gate = parts["a_gate"].reshape(D_MODEL, N_G, 3 * HPG)
    gate = jnp.pad(gate, ((0, 0), (0, 0), (0, GATE_PAD - 3 * HPG))).reshape(D_MODEL, N_G * GATE_PAD)
    w32 = jnp.concatenate([parts["c_f"], parts["a_kc"], parts["a_vc"], gate], axis=1)
    return w16, w32


def _block_diag_heads(w):
    n = CMP_LANES // DH
    zero = jnp.zeros_like(w)
    rows = [jnp.concatenate([w if c == r else zero for c in range(n)], axis=2) for r in range(n)]
    return jnp.concatenate(rows, axis=1)


def _overlap_matrix(seq):
    n_c = seq // CMP_STRIDE
    n_s = seq // SEL_BLOCK
    c_start = jnp.arange(n_c) * CMP_STRIDE
    s_start = jnp.arange(n_s) * SEL_BLOCK
    ov = jnp.clip(jnp.minimum(c_start[None, :] + CMP_BLOCK, s_start[:, None] + SEL_BLOCK)
                  - jnp.maximum(c_start[None, :], s_start[:, None]), 0).astype(F32) / CMP_BLOCK
    return ov.astype(BF16)


def kernel(x, pre_norm, w_in, cmp_pe_k, cmp_pe_v, cmp_w_k, cmp_w_v, pool_w, pool_scale, hgrn_lb_logits, hgrn_norm,
           w_branch, w_out, post_norm):
    batch, seq, _ = x.shape
    depth = w_in.shape[0]
    m = batch * seq
    probs = jax.nn.softmax(hgrn_lb_logits.astype(F32), axis=0)
    lbs = jnp.cumsum(probs, axis=0) - probs[0]
    ov = _overlap_matrix(seq)
    tm_in = min(2048, m)
    tt = min(512, seq)

    w_in16 = _w_in_to_bf16(w_in)
    x2 = x.reshape(m, D_MODEL)
    for l in range(depth):
        w16, w32 = _split_w_in(w_in16[l])
        gain = pre_norm[l].reshape(1, D_MODEL)
        z16 = _inproj(x2, gain, w16, BF16, tm_in, 1024)
        z32 = _inproj(x2, gain, w32, F32, tm_in, 1024)

        pe = jnp.stack([jnp.tile(cmp_pe_k[l], (1, CMP_LANES // DH)), jnp.tile(cmp_pe_v[l], (1, CMP_LANES // DH))])
        wbd = jnp.stack([_block_diag_heads(cmp_w_k[l]), _block_diag_heads(cmp_w_v[l])]).astype(BF16)
        ckv, ckvt = _compress(z32, pe, wbd, batch, seq)
        oc, qaug = _cmp_select(z16, z32, ckv, ckvt, ov, batch, seq)
        kaug, vaug, kwn, vwaug = _kv_prep(z16, batch, seq, tt)
        oa = _nsa_attn(qaug, z32, oc, kaug, vaug, kwn, vwaug, batch, seq, min(SEL_TK, seq // 2))

        ob = _pool(z16, pool_w[l].astype(BF16), pool_scale[l].reshape(1, D_MODEL), batch, seq, tt)
        ocn = _hgrn(z16, z32, lbs[l].reshape(HG_HEADS // HG_NH, 1, HG_NH * HG_DK),
                    hgrn_norm[l].reshape(HG_HEADS // HG_NH, 1, HG_NH * HG_DK), batch, seq, tt, HG_NH)

        x2 = _outproj(oa.reshape(m, D_MODEL), z16, ob.reshape(m, D_MODEL), ocn.reshape(m, D_MODEL), x2,
                      w_branch[l].astype(BF16), w_out[l].astype(BF16), post_norm[l].reshape(1, D_MODEL), 512)
    return x2.reshape(batch, seq, D_MODEL)
```

```python
import functools

import jax
import jax.numpy as jnp
from jax import lax
from jax.experimental import pallas as pl
from jax.experimental.pallas import tpu as pltpu

F32 = jnp.float32
BF16 = jnp.bfloat16

D_MODEL = 1024
NORM_EPS = 1e-6
DH = 64
N_G = 4
HPG = 4
GW = HPG * DH
KVW = N_G * DH
CMP_BLOCK = 32
CMP_STRIDE = 16
CMP_LANES = 128
SEL_BLOCK = 64
N_SELECT = 16
WINDOW = 512
QT = 512
SEL_TK = 512
CQ = 1024
CMP_ROW_LEVELS = 4
LANES = 128
FORCE_SCORE = 1.0e4
N_FORCED = 3
WPAD = 16
ATT_SCALE = DH ** -0.5
MASK_BIAS = -(2.0 ** 30)
LOG2E = 1.4426950408889634
POOL_WINDOWS = (2, 4, 8, 16)
POOL_GROUP = D_MODEL // len(POOL_WINDOWS)
POOL_HALO = 16
HG_HEADS = 8
HG_DK = 128
HG_CHUNK = 64
HG_SUB = 8
HG_NH = 8

Z16_Q, Z16_KS, Z16_VS, Z16_KW, Z16_VW = 0, 1024, 1280, 1536, 1792
Z16_ASILU, Z16_BV, Z16_BSILU = 2048, 3072, 4096
Z16_CQ, Z16_CI, Z16_CSILU, Z16_MERGE = 5120, 6144, 7168, 8192
Z16_W = 11264
Z32_CF, Z32_KC, Z32_VC, Z32_GATE = 0, 1024, 1280, 1536
Z32_W = 2048
GATE_PAD = 128

VMEM_LIMIT = 56 * 1024 * 1024


def _cparams(sem):
    return pltpu.CompilerParams(dimension_semantics=sem, vmem_limit_bytes=VMEM_LIMIT)


def _sigmoid(x):
    return 1.0 / (1.0 + jnp.exp(-x))


def _silu(x):
    return x * _sigmoid(x)


def _inproj_kernel(x_ref, g_ref, w_ref, o_ref, h_ref):
    @pl.when(pl.program_id(1) == 0)
    def _():
        x = x_ref[...]
        ms = jnp.mean(x * x, axis=-1, keepdims=True)
        h_ref[...] = (x * lax.rsqrt(ms + NORM_EPS) * g_ref[...]).astype(BF16)

    o_ref[...] = jnp.dot(h_ref[...], w_ref[...], preferred_element_type=F32).astype(o_ref.dtype)


def _inproj(x2, gain, w, out_dtype, tm, tn):
    m, d = x2.shape
    n = w.shape[1]
    return pl.pallas_call(
        _inproj_kernel,
        grid=(m // tm, n // tn),
        in_specs=[pl.BlockSpec((tm, d), lambda i, j: (i, 0)),
                  pl.BlockSpec((1, d), lambda i, j: (0, 0)),
                  pl.BlockSpec((d, tn), lambda i, j: (0, j))],
        out_specs=pl.BlockSpec((tm, tn), lambda i, j: (i, j)),
        out_shape=jax.ShapeDtypeStruct((m, n), out_dtype),
        scratch_shapes=[pltpu.VMEM((tm, d), BF16)],
        compiler_params=_cparams(("parallel", "arbitrary")),
        name="inproj",
    )(x2, gain, w)


def _compress_kernel(x_ref, pe_ref, w_ref, o_ref, ot_ref, *, nch):
    acc_a = jnp.zeros((nch, CMP_LANES), F32)
    acc_b = jnp.zeros((nch, CMP_LANES), F32)
    for p in range(CMP_STRIDE):
        xp = x_ref[0, pl.ds(p, nch, stride=CMP_STRIDE), :]
        a = (xp + pe_ref[0, p:p + 1, :]).astype(BF16)
        b = (xp + pe_ref[0, CMP_STRIDE + p:CMP_STRIDE + p + 1, :]).astype(BF16)
        acc_a += jnp.dot(a, w_ref[0, p], preferred_element_type=F32)
        acc_b += jnp.dot(b, w_ref[0, CMP_STRIDE + p], preferred_element_type=F32)
    c = acc_a + pltpu.roll(acc_b, nch - 1, 0)
    ct = c.T
    ck = c * LOG2E
    for g in range(CMP_LANES // DH):
        o_ref[0, 0, g] = ck[:, g * DH:(g + 1) * DH].astype(BF16)
        ot_ref[0, 0, g] = ct[g * DH:(g + 1) * DH, :].astype(BF16)


def _compress(z32, pe, wbd, batch, seq):
    nch = seq // CMP_STRIDE
    z3 = z32.reshape(batch, seq, Z32_W)
    gpb = CMP_LANES // DH
    return pl.pallas_call(
        functools.partial(_compress_kernel, nch=nch),
        grid=(batch, 2, N_G // gpb),
        in_specs=[pl.BlockSpec((1, seq, CMP_LANES),
                               lambda b, s, c: (b, 0, (Z32_KC + s * KVW) // CMP_LANES + c)),
                  pl.BlockSpec((1, CMP_BLOCK, CMP_LANES), lambda b, s, c: (s, 0, 0)),
                  pl.BlockSpec((1, CMP_BLOCK, CMP_LANES, CMP_LANES), lambda b, s, c: (s, 0, 0, 0))],
        out_specs=[pl.BlockSpec((1, 1, gpb, nch, DH), lambda b, s, c: (b, s, c, 0, 0)),
                   pl.BlockSpec((1, 1, gpb, DH, nch), lambda b, s, c: (b, s, c, 0, 0))],
        out_shape=[jax.ShapeDtypeStruct((batch, 2, N_G, nch, DH), BF16),
                   jax.ShapeDtypeStruct((batch, 2, N_G, DH, nch), BF16)],
        compiler_params=_cparams(("parallel", "parallel", "parallel")),
        name="compress",
    )(z3, pe, wbd)


def _cmp_select_kernel(q_ref, ck_ref, cvt_ref, ov_ref, gate_ref, oc_ref, qa_ref, imp_ref, *, nch, n_s, n_sel):
    s0 = pl.program_id(2) * CQ
    qt = (q_ref[0].astype(F32).T * ATT_SCALE).astype(BF16)
    t_lane = s0 + lax.broadcasted_iota(jnp.int32, (1, CQ), 1)
    n_vis = (s0 + CQ - CMP_BLOCK) // CMP_STRIDE + 1

    def attend(rows):
        ck = ck_ref[0, 0, 0, 0:rows, :]
        cvt = cvt_ref[0, 0, 0, :, 0:rows]
        c_end = lax.broadcasted_iota(jnp.int32, (rows, 1), 0) * CMP_STRIDE + (CMP_BLOCK - 1)
        visible = c_end <= t_lane
        gates = _sigmoid(gate_ref[0])
        p_sum = jnp.zeros((rows, CQ), F32)
        outs = []
        for h in range(HPG):
            s = jnp.dot(ck, qt[h * DH:(h + 1) * DH, :], preferred_element_type=F32)
            s = jnp.where(visible, s, -jnp.inf)
            m = jnp.max(s, axis=0, keepdims=True)
            m = jnp.where(m == -jnp.inf, 0.0, m)
            e = jnp.exp2(s - m)
            p = e / jnp.maximum(jnp.sum(e, axis=0, keepdims=True), 1e-30)
            p_sum += p
            oct_h = jnp.dot(cvt, p.astype(BF16), preferred_element_type=F32)
            outs.append(oct_h.T * gates[:, 3 * h:3 * h + 1])
        oc_ref[0] = jnp.concatenate(outs, axis=1).astype(BF16)
        p_hi = p_sum.astype(BF16)
        p_lo = (p_sum - p_hi.astype(F32)).astype(BF16)
        ov = ov_ref[:, 0:rows]
        imp_ref[...] = (jnp.dot(ov, p_hi, preferred_element_type=F32)
                        + jnp.dot(ov, p_lo, preferred_element_type=F32))

    quantum = nch // CMP_ROW_LEVELS
    for j in range(CMP_ROW_LEVELS):
        rows = quantum * (j + 1)
        if j == 0:
            cond = n_vis <= rows
        else:
            cond = (n_vis > rows - quantum) & (n_vis <= rows)
        pl.when(cond)(functools.partial(attend, rows))

    blk = lax.broadcasted_iota(jnp.int32, (n_s, 1), 0)

    def pick(_, sc):
        m = jnp.max(sc, axis=0, keepdims=True)
        first = jnp.min(jnp.where(sc == m, blk, n_s), axis=0, keepdims=True)
        return jnp.where(blk == first, -jnp.inf, sc)

    biases = []
    for j in range(CQ // LANES):
        lanes = slice(j * LANES, (j + 1) * LANES)
        cur = t_lane[:, lanes] // SEL_BLOCK
        valid = blk <= cur
        force = (blk == 0) | (blk == cur) | (blk == cur - 1)
        score = jnp.where(valid, jnp.where(force, -jnp.inf, imp_ref[:, lanes]), -1.0)
        sc = lax.fori_loop(0, max(n_sel - N_FORCED, 0), pick, score, unroll=True)
        biases.append(jnp.where((sc == -jnp.inf) & valid, 0.0, MASK_BIAS).astype(BF16))
    bias_t = jnp.concatenate(biases, axis=1)
    flag_rows = jnp.where(lax.broadcasted_iota(jnp.int32, (WPAD, HPG * QT), 0) == 0, MASK_BIAS, 0.0).astype(BF16)
    for j in range(CQ // QT):
        lanes = slice(j * QT, (j + 1) * QT)
        qa_ref[0, 0, j, 0:n_s, :] = jnp.concatenate([bias_t[:, lanes]] * HPG, axis=1)
        qa_ref[0, 0, j, n_s:n_s + DH, :] = jnp.concatenate(
            [qt[h * DH:(h + 1) * DH, lanes] for h in range(HPG)], axis=1)
        qa_ref[0, 0, j, n_s + DH:n_s + DH + WPAD, :] = flag_rows


def _cmp_select(z16, z32, ck, cvt, ov, batch, seq):
    nch = seq // CMP_STRIDE
    n_s = seq // SEL_BLOCK
    n_sel = min(N_SELECT, n_s)
    z3 = z16.reshape(batch, seq, Z16_W)
    g3 = z32.reshape(batch, seq, Z32_W)
    return pl.pallas_call(
        functools.partial(_cmp_select_kernel, nch=nch, n_s=n_s, n_sel=n_sel),
        grid=(batch, N_G, seq // CQ),
        in_specs=[pl.BlockSpec((1, CQ, GW), lambda b, g, i: (b, i, Z16_Q // GW + g)),
                  pl.BlockSpec((1, 1, 1, nch, DH), lambda b, g, i: (b, 0, g, 0, 0)),
                  pl.BlockSpec((1, 1, 1, DH, nch), lambda b, g, i: (b, 1, g, 0, 0)),
                  pl.BlockSpec((n_s, nch), lambda b, g, i: (0, 0)),
                  pl.BlockSpec((1, CQ, GATE_PAD), lambda b, g, i: (b, i, Z32_GATE // GATE_PAD + g))],
        out_specs=[pl.BlockSpec((1, CQ, GW), lambda b, g, i: (b, i, g)),
                   pl.BlockSpec((1, 1, CQ // QT, n_s + DH + WPAD, HPG * QT), lambda b, g, i: (b, g, i, 0, 0))],
        out_shape=[jax.ShapeDtypeStruct((batch, seq, D_MODEL), BF16),
                   jax.ShapeDtypeStruct((batch, N_G, seq // QT, n_s + DH + WPAD, HPG * QT), BF16)],
        scratch_shapes=[pltpu.VMEM((n_s, CQ), F32)],
        compiler_params=_cparams(("parallel", "parallel", "parallel")),
        name="cmp_select",
    )(z3, ck, cvt, ov, g3)


def _kv_prep_kernel(ks_ref, vs_ref, kw_ref, vw_ref, ka_ref, va_ref, kwn_ref, vwa_ref, *, n_s, tt):
    is_pad = pl.program_id(1) == 0
    t0 = jnp.maximum(pl.program_id(1) - 1, 0) * tt
    ks = ks_ref[0]
    kw = jnp.where(is_pad, 1.0, kw_ref[0].astype(F32)).astype(BF16)
    vst = vs_ref[0].astype(F32).T
    vwt = jnp.where(is_pad, 0.0, vw_ref[0].astype(F32).T)
    pos = t0 + lax.broadcasted_iota(jnp.int32, (tt, 1), 0)
    blk = lax.broadcasted_iota(jnp.int32, (1, n_s), 1)
    onehot = jnp.where(pos // SEL_BLOCK == blk, 1.0, 0.0).astype(BF16)
    ones_row = jnp.where(lax.broadcasted_iota(jnp.int32, (DH, tt), 0) == 0, 1.0, 0.0).astype(BF16)
    no_flag = jnp.zeros((tt, WPAD), BF16)
    w_flag = jnp.where(is_pad, 1.0, jnp.zeros((tt, WPAD), F32)).astype(BF16)
    for g in range(N_G):
        ka_ref[0, g, :, 0:n_s] = onehot
        ka_ref[0, g, :, n_s:n_s + DH] = ks[:, g * DH:(g + 1) * DH]
        ka_ref[0, g, :, n_s + DH:n_s + DH + WPAD] = no_flag
        kwn_ref[0, g, :, 0:DH] = kw[:, g * DH:(g + 1) * DH]
        kwn_ref[0, g, :, DH:DH + WPAD] = w_flag
        va_ref[0, g, 0:DH, :] = vst[g * DH:(g + 1) * DH, :].astype(BF16)
        va_ref[0, g, DH:2 * DH, :] = ones_row
        vwa_ref[0, g, 0:DH, :] = vwt[g * DH:(g + 1) * DH, :].astype(BF16)
        vwa_ref[0, g, DH:2 * DH, :] = ones_row


def _kv_prep(z16, batch, seq, tt):
    n_s = seq // SEL_BLOCK
    z3 = z16.reshape(batch, seq, Z16_W)

    assert tt == WINDOW, (tt, WINDOW)

    def tile(i):
        return jnp.maximum(i - 1, 0)

    def col(off):
        return pl.BlockSpec((1, tt, KVW), lambda b, i: (b, tile(i), off // KVW))

    return pl.pallas_call(
        functools.partial(_kv_prep_kernel, n_s=n_s, tt=tt),
        grid=(batch, seq // tt + 1),
        in_specs=[col(Z16_KS), col(Z16_VS), col(Z16_KW), col(Z16_VW)],
        out_specs=[pl.BlockSpec((1, N_G, tt, n_s + DH + WPAD), lambda b, i: (b, 0, tile(i), 0)),
                   pl.BlockSpec((1, N_G, 2 * DH, tt), lambda b, i: (b, 0, 0, tile(i))),
                   pl.BlockSpec((1, N_G, tt, DH + WPAD), lambda b, i: (b, 0, i, 0)),
                   pl.BlockSpec((1, N_G, 2 * DH, tt), lambda b, i: (b, 0, 0, i))],
        out_shape=[jax.ShapeDtypeStruct((batch, N_G, seq, n_s + DH + WPAD), BF16),
                   jax.ShapeDtypeStruct((batch, N_G, 2 * DH, seq), BF16),
                   jax.ShapeDtypeStruct((batch, N_G, seq + WINDOW, DH + WPAD), BF16),
                   jax.ShapeDtypeStruct((batch, N_G, 2 * DH, seq + WINDOW), BF16)],
        compiler_params=_cparams(("parallel", "arbitrary")),
        name="kv_prep",
    )(z3, z3, z3, z3)


def _nsa_attn_kernel(qa_ref, gate_ref, oc_ref, k_ref, v_ref, kw_ref, vw_ref, o_ref,
                     sa_ref, sb_ref, ma_ref, mb_ref, m_ref, acc_ref, accw_ref, *, tk, n_s, span):
    s0 = pl.program_id(2) * QT
    unit = 2 * tk
    n_units = s0 // unit
    u0 = pl.multiple_of(n_units * unit, unit)
    m_ref[...] = jnp.full(m_ref.shape, -jnp.inf, F32)
    acc_ref[...] = jnp.zeros(acc_ref.shape, F32)

    w0 = pl.multiple_of(s0, QT)
    sw = jnp.dot(kw_ref[0, 0, pl.ds(w0, span), :], qa_ref[0, 0, 0, n_s:n_s + DH + WPAD, :],
                 preferred_element_type=F32)
    q_loc = lax.broadcasted_iota(jnp.int32, (1, HPG * QT), 1) % QT
    r_loc = lax.broadcasted_iota(jnp.int32, (QT, 1), 0)
    slabs = [(0, jnp.where(r_loc > q_loc, sw[0:QT], -jnp.inf))]
    if WINDOW > QT:
        slabs.append((QT, sw[QT:WINDOW]))
    slabs.append((WINDOW, jnp.where(r_loc <= q_loc, sw[WINDOW:WINDOW + QT], -jnp.inf)))
    mw = functools.reduce(jnp.maximum, [jnp.max(s, axis=0, keepdims=True) for _, s in slabs])
    accw_ref[...] = functools.reduce(
        lambda x, y: x + y,
        [jnp.dot(vw_ref[0, 0, :, pl.ds(pl.multiple_of(w0 + r, QT), s.shape[0])], jnp.exp2(s - mw).astype(BF16),
                 preferred_element_type=F32) for r, s in slabs])

    def scores(off):
        return jnp.dot(k_ref[0, 0, pl.ds(off, tk), :], qa_ref[0, 0, 0], preferred_element_type=F32)

    def stage(s_ref, mx_ref, off):
        s = scores(off)
        s_ref[...] = s
        mx_ref[...] = jnp.max(s, axis=0, keepdims=True)

    def accumulate(s, tile_max, off):
        m_prev = m_ref[...]
        m_new = jnp.maximum(m_prev, tile_max)
        p = jnp.exp2(s - m_new).astype(BF16)
        acc_ref[...] = jnp.exp2(m_prev - m_new) * acc_ref[...] + jnp.dot(
            v_ref[0, 0, :, pl.ds(off, tk)], p, preferred_element_type=F32)
        m_ref[...] = m_new

    stage(sa_ref, ma_ref, 0)

    s_own = jnp.where(r_loc <= q_loc, scores(pl.multiple_of(s0, tk)), MASK_BIAS)
    accumulate(s_own, jnp.max(s_own, axis=0, keepdims=True), pl.multiple_of(s0, tk))

    @pl.when(s0 >= u0 + tk)
    def _():
        s_odd = scores(u0)
        accumulate(s_odd, jnp.max(s_odd, axis=0, keepdims=True), u0)

    @pl.when(n_units > 0)
    def _():
        def body(u, carry):
            off = pl.multiple_of(u * unit, unit)
            stage(sb_ref, mb_ref, off + tk)
            accumulate(sa_ref[...], ma_ref[...], off)
            nxt = pl.multiple_of(jnp.minimum(u + 1, n_units - 1) * unit, unit)
            stage(sa_ref, ma_ref, nxt)
            accumulate(sb_ref[...], mb_ref[...], off + tk)
            return carry

        lax.fori_loop(0, n_units, body, 0)

    gates_t = _sigmoid(gate_ref[0]).T
    outs = []
    for h in range(HPG):
        a = acc_ref[:, h * QT:(h + 1) * QT]
        aw = accw_ref[:, h * QT:(h + 1) * QT]
        w_sel = gates_t[3 * h + 1:3 * h + 2, :] / a[DH:DH + 1, :]
        w_win = gates_t[3 * h + 2:3 * h + 3, :] / aw[DH:DH + 1, :]
        outs.append((a[0:DH, :] * w_sel + aw[0:DH, :] * w_win).T)
    o_ref[0] = (oc_ref[0].astype(F32) + jnp.concatenate(outs, axis=1)).astype(BF16)


def _nsa_attn(qaug, z32, oc, kaug, vaug, kwn, vwaug, batch, seq, tk):
    assert tk == QT and seq % (2 * tk) == 0, (tk, QT, seq)
    kd = kaug.shape[-1]
    span = WINDOW + QT
    g3 = z32.reshape(batch, seq, Z32_W)

    def per_head(shape):
        return pl.BlockSpec((1, 1) + shape, lambda b, g, i: (b, g, 0, 0))

    return pl.pallas_call(
        functools.partial(_nsa_attn_kernel, tk=tk, n_s=kd - DH - WPAD, span=span),
        grid=(batch, N_G, seq // QT),
        in_specs=[pl.BlockSpec((1, 1, 1, kd, HPG * QT), lambda b, g, i: (b, g, i, 0, 0)),
                  pl.BlockSpec((1, QT, GATE_PAD), lambda b, g, i: (b, i, Z32_GATE // GATE_PAD + g)),
                  pl.BlockSpec((1, QT, GW), lambda b, g, i: (b, i, g)),
                  per_head((seq, kd)), per_head((2 * DH, seq)),
                  per_head((seq + WINDOW, DH + WPAD)), per_head((2 * DH, seq + WINDOW))],
        out_specs=pl.BlockSpec((1, QT, GW), lambda b, g, i: (b, i, g)),
        out_shape=jax.ShapeDtypeStruct((batch, seq, D_MODEL), BF16),
        scratch_shapes=[pltpu.VMEM((tk, HPG * QT), F32),
                        pltpu.VMEM((tk, HPG * QT), F32),
                        pltpu.VMEM((1, HPG * QT), F32),
                        pltpu.VMEM((1, HPG * QT), F32),
                        pltpu.VMEM((1, HPG * QT), F32),
                        pltpu.VMEM((2 * DH, HPG * QT), F32),
                        pltpu.VMEM((2 * DH, HPG * QT), F32)],
        compiler_params=_cparams(("parallel", "parallel", "arbitrary")),
        name="nsa_attn",
    )(qaug, g3, oc, kaug, vaug, kwn, vwaug)


def _pool_kernel(v_ref, prev_ref, gate_ref, w_ref, scale_ref, o_ref, *, tt):
    i = pl.program_id(1)
    cur = v_ref[0].astype(F32)
    prev = jnp.where(i > 0, prev_ref[0].astype(F32), 0.0)
    t = i * tt + lax.broadcasted_iota(jnp.int32, (tt, 1), 0)
    for g, w in enumerate(POOL_WINDOWS):
        sl = slice(g * POOL_GROUP, (g + 1) * POOL_GROUP)
        ext = jnp.concatenate([prev[:, sl], cur[:, sl]], axis=0)
        acc = ext
        d = 1
        while d < w:
            acc = acc + pltpu.roll(acc, d, 0)
            d *= 2
        cnt = jnp.minimum(t + 1, w).astype(F32)
        pooled = acc[POOL_HALO:, :] / cnt - cur[:, sl]
        mixed = jnp.dot(pooled.astype(BF16), w_ref[g], preferred_element_type=F32)
        o_ref[0, :, sl] = (mixed * scale_ref[:, sl] * _silu(gate_ref[0, :, sl].astype(F32))).astype(BF16)


def _pool(z16, w_pool, scale, batch, seq, tt):
    z3 = z16.reshape(batch, seq, Z16_W)
    hb = tt // POOL_HALO
    return pl.pallas_call(
        functools.partial(_pool_kernel, tt=tt),
        grid=(batch, seq // tt),
        in_specs=[pl.BlockSpec((1, tt, D_MODEL), lambda b, i: (b, i, Z16_BV // D_MODEL)),
                  pl.BlockSpec((1, POOL_HALO, D_MODEL),
                               lambda b, i: (b, jnp.maximum(i * hb - 1, 0), Z16_BV // D_MODEL)),
                  pl.BlockSpec((1, tt, D_MODEL), lambda b, i: (b, i, Z16_BSILU // D_MODEL)),
                  pl.BlockSpec((len(POOL_WINDOWS), POOL_GROUP, POOL_GROUP), lambda b, i: (0, 0, 0)),
                  pl.BlockSpec((1, D_MODEL), lambda b, i: (0, 0))],
        out_specs=pl.BlockSpec((1, tt, D_MODEL), lambda b, i: (b, i, 0)),
        out_shape=jax.ShapeDtypeStruct((batch, seq, D_MODEL), BF16),
        compiler_params=_cparams(("parallel", "parallel")),
        name="pool",
    )(z3, z3, z3, w_pool, scale)


def _hgrn_kernel(q_ref, f_ref, i_ref, gate_ref, lb_ref, gain_ref, o_ref, st_ref, *, n_chunks, nh):
    c, sub = HG_CHUNK, HG_SUB

    @pl.when(pl.program_id(2) == 0)
    def _():
        st_ref[...] = jnp.zeros(st_ref.shape, F32)

    row = lax.broadcasted_iota(jnp.int32, (c, 1), 0)
    tri = jnp.where(lax.broadcasted_iota(jnp.int32, (c, c), 1) <= row, 1.0, 0.0).astype(BF16)
    col_c = lax.broadcasted_iota(jnp.int32, (1, c), 1)
    nt = (((1,), (1,)), ((), ()))

    def head_chunk(r0, hd):
        cols = slice(hd * HG_DK, (hd + 1) * HG_DK)
        lb = lb_ref[0, :, cols]
        f = lb + (1.0 - lb) * _sigmoid(f_ref[0, pl.ds(r0, c), cols])
        lf = jnp.log2(f)
        k = 1.0 - f
        q = q_ref[0, pl.ds(r0, c), cols].astype(F32)
        v = i_ref[0, pl.ds(r0, c), cols]
        hi = lf.astype(BF16)
        r1 = lf - hi.astype(F32)
        mid = r1.astype(BF16)
        lo = (r1 - mid.astype(F32)).astype(BF16)
        a = (jnp.dot(tri, hi, preferred_element_type=F32)
             + jnp.dot(tri, mid, preferred_element_type=F32)
             + jnp.dot(tri, lo, preferred_element_type=F32))

        st = st_ref[hd]
        o = lax.dot_general((q * jnp.exp2(a)).astype(BF16), st.astype(BF16), nt, preferred_element_type=F32)

        att = jnp.zeros((c, c), F32)
        half = c // 2
        while half >= sub:
            pair = 2 * half
            ref = jnp.concatenate(
                [jnp.broadcast_to(a[p * pair + half - 1:p * pair + half, :], (pair, HG_DK))
                 for p in range(c // pair)], axis=0)
            right = (row % pair) >= half
            qt = jnp.where(right, q * jnp.exp2(jnp.minimum(a - ref, 0.0)), 0.0)
            kt = jnp.where(right, 0.0, k * jnp.exp2(jnp.minimum(ref - a, 0.0)))
            lvl = lax.dot_general(qt.astype(BF16), kt.astype(BF16), nt, preferred_element_type=F32)
            att = lvl if pair == c else att + jnp.where((row // pair) == (col_c // pair), lvl, 0.0)
            half //= 2
        q3 = q.reshape(c // sub, sub, HG_DK)
        k3 = k.reshape(c // sub, sub, HG_DK)
        f3 = f.reshape(c // sub, sub, HG_DK)
        dec = None
        diag = jnp.zeros((c, c), F32)
        for d in range(sub):
            if d == 0:
                prod = q3 * k3
            else:
                f_sh = f3 if d == 1 else pltpu.roll(f3, d - 1, 1)
                dec = f_sh if dec is None else dec * f_sh
                prod = q3 * pltpu.roll(k3, d, 1) * dec
            colv = jnp.sum(prod, axis=2, keepdims=True).reshape(c, 1)
            diag = jnp.where((col_c == row - d) & (row % sub >= d), colv, diag)
        o = o + jnp.dot((att + diag).astype(BF16), v, preferred_element_type=F32)

        a_last = a[c - 1:c, :]
        kd = (k * jnp.exp2(a_last - a)).astype(BF16)
        st_ref[hd] = st * jnp.exp2(a_last) + lax.dot_general(
            v, kd, (((0,), (0,)), ((), ())), preferred_element_type=F32)

        ms = jnp.mean(o * o, axis=-1, keepdims=True)
        y = o * lax.rsqrt(ms + NORM_EPS) * gain_ref[0, :, cols]
        o_ref[0, pl.ds(r0, c), cols] = (y * _silu(gate_ref[0, pl.ds(r0, c), cols].astype(F32))).astype(BF16)

    def chunk(n, carry):
        r0 = pl.multiple_of(n * c, c)
        for hd in range(nh):
            head_chunk(r0, hd)
        return carry

    lax.fori_loop(0, n_chunks, chunk, 0, unroll=4)


def _hgrn(z16, z32, lb, gain, batch, seq, tc, nh):
    z3 = z16.reshape(batch, seq, Z16_W)
    f3 = z32.reshape(batch, seq, Z32_W)
    w = nh * HG_DK

    def col16(off):
        return pl.BlockSpec((1, tc, w), lambda b, h, i: (b, i, off // w + h))

    return pl.pallas_call(
        functools.partial(_hgrn_kernel, n_chunks=tc // HG_CHUNK, nh=nh),
        grid=(batch, HG_HEADS // nh, seq // tc),
        in_specs=[col16(Z16_CQ),
                  pl.BlockSpec((1, tc, w), lambda b, h, i: (b, i, Z32_CF // w + h)),
                  col16(Z16_CI), col16(Z16_CSILU),
                  pl.BlockSpec((1, 1, w), lambda b, h, i: (h, 0, 0)),
                  pl.BlockSpec((1, 1, w), lambda b, h, i: (h, 0, 0))],
        out_specs=pl.BlockSpec((1, tc, w), lambda b, h, i: (b, i, h)),
        out_shape=jax.ShapeDtypeStruct((batch, seq, D_MODEL), BF16),
        scratch_shapes=[pltpu.VMEM((nh, HG_DK, HG_DK), F32)],
        compiler_params=_cparams(("parallel", "parallel", "arbitrary")),
        name="hgrn",
    )(z3, f3, z3, z3, lb, gain)


def _outproj_kernel(oa_ref, asilu_ref, ob_ref, ocn_ref, m0_ref, m1_ref, m2_ref,
                    x_ref, wb_ref, wo_ref, post_ref, o_ref):
    oa = (oa_ref[...].astype(F32) * _silu(asilu_ref[...].astype(F32))).astype(BF16)
    merged = _sigmoid(m0_ref[...].astype(F32)) * jnp.dot(oa, wb_ref[0], preferred_element_type=F32)
    merged += _sigmoid(m1_ref[...].astype(F32)) * jnp.dot(ob_ref[...], wb_ref[1], preferred_element_type=F32)
    merged += _sigmoid(m2_ref[...].astype(F32)) * jnp.dot(ocn_ref[...], wb_ref[2], preferred_element_type=F32)
    out = jnp.dot(merged.astype(BF16), wo_ref[...], preferred_element_type=F32)
    ms = jnp.mean(out * out, axis=-1, keepdims=True)
    o_ref[...] = x_ref[...] + out * lax.rsqrt(ms + NORM_EPS) * post_ref[...]


def _outproj(oa, z16, ob, ocn, x2, wb, wo, post, tm):
    m = x2.shape[0]

    def rows(j=0):
        return pl.BlockSpec((tm, D_MODEL), lambda i: (i, j))

    return pl.pallas_call(
        _outproj_kernel,
        grid=(m // tm,),
        in_specs=[rows(), rows(Z16_ASILU // D_MODEL), rows(), rows(),
                  rows(Z16_MERGE // D_MODEL), rows(Z16_MERGE // D_MODEL + 1), rows(Z16_MERGE // D_MODEL + 2),
                  rows(),
                  pl.BlockSpec((3, D_MODEL, D_MODEL), lambda i: (0, 0, 0)),
                  pl.BlockSpec((D_MODEL, D_MODEL), lambda i: (0, 0)),
                  pl.BlockSpec((1, D_MODEL), lambda i: (0, 0))],
        out_specs=rows(),
        out_shape=jax.ShapeDtypeStruct((m, D_MODEL), F32),
        compiler_params=_cparams(("parallel",)),
        name="outproj",
    )(oa, z16, ob, ocn, z16, z16, z16, x2, wb, wo, post)


W_IN_SIZES = (1024, 256, 256, 256, 256, 256, 256, 3 * N_G * HPG, 1024, 1024, 1024, 1024, 1024, 1024, 1024, 3072)
W_IN_NAMES = ("a_q", "a_kc", "a_vc", "a_ks", "a_vs", "a_kw", "a_vw", "a_gate", "a_silu", "b_v", "b_silu",
              "c_q", "c_f", "c_i", "c_silu", "merge")


def _w_in_to_bf16(w_in):
    scale, off = [], 0
    for n, s in zip(W_IN_NAMES, W_IN_SIZES):
        scale += [LOG2E if n in ("a_ks", "a_kw") else 1.0] * s
        off += s
    return (w_in * jnp.asarray(scale, F32)).astype(BF16)


def _split_w_in(w):
    parts, off = {}, 0
    for n, s in zip(W_IN_NAMES, W_IN_SIZES):
        parts[n] = w[:, off:off + s]
        off += s
    w16 = jnp.concatenate([parts[n] for n in ("a_q", "a_ks", "a_vs", "a_kw", "a_vw", "a_silu", "b_v", "b_silu",
                                              "c_q", "c_i", "c_silu", "merge")], axis=1)
    gate = parts["a_gate"].reshape(D_MODEL, N_G, 3 * HPG)
    gate = jnp.pad(gate, ((0, 0), (0, 0), (0, GATE_PAD - 3 * HPG))).reshape(D_MODEL, N_G * GATE_PAD)
    w32 = jnp.concatenate([parts["c_f"], parts["a_kc"], parts["a_vc"], gate], axis=1)
    return w16, w32


def _block_diag_heads(w):
    n = CMP_LANES // DH
    zero = jnp.zeros_like(w)
    rows = [jnp.concatenate([w if c == r else zero for c in range(n)], axis=2) for r in range(n)]
    return jnp.concatenate(rows, axis=1)


def _overlap_matrix(seq):
    n_c = seq // CMP_STRIDE
    n_s = seq // SEL_BLOCK
    c_start = jnp.arange(n_c) * CMP_STRIDE
    s_start = jnp.arange(n_s) * SEL_BLOCK
    ov = jnp.clip(jnp.minimum(c_start[None, :] + CMP_BLOCK, s_start[:, None] + SEL_BLOCK)
                  - jnp.maximum(c_start[None, :], s_start[:, None]), 0).astype(F32) / CMP_BLOCK
    return ov.astype(BF16)


def kernel(x, pre_norm, w_in, cmp_pe_k, cmp_pe_v, cmp_w_k, cmp_w_v, pool_w, pool_scale, hgrn_lb_logits, hgrn_norm,
           w_branch, w_out, post_norm):
    batch, seq, _ = x.shape
    depth = w_in.shape[0]
    m = batch * seq
    probs = jax.nn.softmax(hgrn_lb_logits.astype(F32), axis=0)
    lbs = jnp.cumsum(probs, axis=0) - probs[0]
    ov = _overlap_matrix(seq)
    tm_in = min(2048, m)
    tt = min(512, seq)

    w_in16 = _w_in_to_bf16(w_in)
    x2 = x.reshape(m, D_MODEL)
    for l in range(depth):
        w16, w32 = _split_w_in(w_in16[l])
        gain = pre_norm[l].reshape(1, D_MODEL)
        z16 = _inproj(x2, gain, w16, BF16, tm_in, 1024)
        z32 = _inproj(x2, gain, w32, F32, tm_in, 1024)

        pe = jnp.stack([jnp.tile(cmp_pe_k[l], (1, CMP_LANES // DH)), jnp.tile(cmp_pe_v[l], (1, CMP_LANES // DH))])
        wbd = jnp.stack([_block_diag_heads(cmp_w_k[l]), _block_diag_heads(cmp_w_v[l])]).astype(BF16)
        ckv, ckvt = _compress(z32, pe, wbd, batch, seq)
        oc, qaug = _cmp_select(z16, z32, ckv, ckvt, ov, batch, seq)
        kaug, vaug, kwn, vwaug = _kv_prep(z16, batch, seq, tt)
        oa = _nsa_attn(qaug, z32, oc, kaug, vaug, kwn, vwaug, batch, seq, min(SEL_TK, seq // 2))

        ob = _pool(z16, pool_w[l].astype(BF16), pool_scale[l].reshape(1, D_MODEL), batch, seq, tt)
        ocn = _hgrn(z16, z32, lbs[l].reshape(HG_HEADS // HG_NH, 1, HG_NH * HG_DK),
                    hgrn_norm[l].reshape(HG_HEADS // HG_NH, 1, HG_NH * HG_DK), batch, seq, tt, HG_NH)

        x2 = _outproj(oa.reshape(m, D_MODEL), z16, ob.reshape(m, D_MODEL), ocn.reshape(m, D_MODEL), x2,
                      w_branch[l].astype(BF16), w_out[l].astype(BF16), post_norm[l].reshape(1, D_MODEL), 512)
    return x2.reshape(batch, seq, D_MODEL)
```
